```python
import jax
import jax.numpy as jnp
from jax import lax
import numpy as np

D_MODEL = 4096
BATCH = 2
SEQ = 4096
DEPTH = 4

N_MIXERS = 2
N_ATTN_LAYERS = (DEPTH + 1) // 2
N_RET_LAYERS = DEPTH // 2

ATTN_HEAD_DIM = 64
ATTN_Q_HEADS = D_MODEL // ATTN_HEAD_DIM
ATTN_KV_HEADS = 8
ATTN_GROUP = ATTN_Q_HEADS // ATTN_KV_HEADS
Q_DIM = ATTN_Q_HEADS * ATTN_HEAD_DIM
KV_DIM = ATTN_KV_HEADS * ATTN_HEAD_DIM
ATTN_QKV_DIM = Q_DIM + 2 * KV_DIM
WINDOW = 128
BLOCK = 128
ROPE_THETA = 500000.0
ROPE_DIM = ATTN_HEAD_DIM // 4

RET_HEADS = 16
RET_KEY_DIM = D_MODEL // RET_HEADS
RET_VALUE_FACTOR = 2
RET_VAL_DIM = RET_VALUE_FACTOR * RET_KEY_DIM
RET_V_WIDTH = RET_HEADS * RET_VAL_DIM
RET_PROJ_DIM = 2 * D_MODEL + 2 * RET_V_WIDTH
RET_CHUNK = 128
RET_ROT_BASE = 10000.0
GN_EPS = 1e-6

N_EXPERTS = 32
TOP_K = 4
EXPERT_FF = 256
SWIGLU_LIMIT = 7.0
SWIGLU_ALPHA = 1.702

DEEPNORM_ALPHA = (2 * DEPTH) ** 0.25
DEEPNORM_BETA = (8 * DEPTH) ** -0.25
LN_EPS = 1e-5
N_MOD = 6
ADALN_INIT_SCALE = 0.1
MAX_POS_OFFSET = 1024

kernel_name = 'hybrid_swa_sink_retention_moe_deepnorm_adaln'


def _layer_norm(x, gain, bias):
    xf = x.astype(jnp.float32)
    mu = jnp.mean(xf, axis=-1, keepdims=True)
    var = jnp.mean(jnp.square(xf - mu), axis=-1, keepdims=True)
    y = (xf - mu) * lax.rsqrt(var + LN_EPS) * gain.astype(jnp.float32) + bias.astype(jnp.float32)
    return y.astype(x.dtype)


def _rope_partial(x, cos, sin):
    half = ROPE_DIM // 2
    x1 = x[..., :half]
    x2 = x[..., half:ROPE_DIM]
    return jnp.concatenate([x1 * cos - x2 * sin, x2 * cos + x1 * sin, x[..., ROPE_DIM:]], axis=-1)


def _rotate_interleaved(x, cos, sin):
    xr = x.reshape(x.shape[:-1] + (x.shape[-1] // 2, 2))
    x0, x1 = xr[..., 0], xr[..., 1]
    return jnp.stack([x0 * cos - x1 * sin, x1 * cos + x0 * sin], axis=-1).reshape(x.shape)


def _window_mask(n_blocks):
    qi = jnp.arange(BLOCK)[:, None]
    kj = jnp.arange(2 * BLOCK)[None, :]
    diff = qi + BLOCK - kj
    band = (diff >= 0) & (diff < WINDOW)
    has_prev = (jnp.arange(n_blocks) > 0)[:, None, None]
    return band[None] & (has_prev | (kj >= BLOCK)[None])


def _sliding_window_sink_attention(u, w_qkv, b_qkv, sinks, w_o, b_o, cos, sin):
    bsz, seq, _ = u.shape
    nb = seq // BLOCK
    qkv = u @ w_qkv + b_qkv
    q, k, v = jnp.split(qkv, [Q_DIM, Q_DIM + KV_DIM], axis=-1)
    q = _rope_partial(q.reshape(bsz, seq, ATTN_Q_HEADS, ATTN_HEAD_DIM), cos, sin)
    k = _rope_partial(k.reshape(bsz, seq, ATTN_KV_HEADS, ATTN_HEAD_DIM), cos, sin)
    q = q.reshape(bsz, nb, BLOCK, ATTN_KV_HEADS, ATTN_GROUP, ATTN_HEAD_DIM)

    def band_keys(t):
        t = t.reshape(bsz, nb, BLOCK, ATTN_KV_HEADS, ATTN_HEAD_DIM)
        prev = jnp.concatenate([jnp.zeros_like(t[:, :1]), t[:, :-1]], axis=1)
        return jnp.concatenate([prev, t], axis=2)

    kb = band_keys(k)
    vb = band_keys(v)
    s = jnp.einsum('bnqhgd,bnkhd->bnhgqk', q, kb).astype(jnp.float32) * (ATTN_HEAD_DIM ** -0.5)
    s = jnp.where(_window_mask(nb)[None, :, None, None], s, -jnp.inf)
    sink = jnp.broadcast_to(
        sinks.astype(jnp.float32).reshape(1, 1, ATTN_KV_HEADS, ATTN_GROUP, 1, 1), s.shape[:-1] + (1,))
    p = jax.nn.softmax(jnp.concatenate([s, sink], axis=-1), axis=-1)[..., :-1]
    o = jnp.einsum('bnhgqk,bnkhd->bnqhgd', p.astype(vb.dtype), vb)
    return o.reshape(bsz, seq, Q_DIM) @ w_o + b_o


def _retention(u, w_qkvg, gn_gain, w_o, cos, sin):
    bsz, seq, _ = u.shape
    nc = seq // RET_CHUNK
    C = RET_CHUNK
    proj = u @ w_qkvg
    q, k, v, g = jnp.split(proj, [D_MODEL, 2 * D_MODEL, 2 * D_MODEL + RET_V_WIDTH], axis=-1)
    q = _rotate_interleaved(q.reshape(bsz, seq, RET_HEADS, RET_KEY_DIM), cos, sin)
    k = _rotate_interleaved(k.reshape(bsz, seq, RET_HEADS, RET_KEY_DIM), cos, sin) * (RET_KEY_DIM ** -0.5)
    qc = q.reshape(bsz, nc, C, RET_HEADS, RET_KEY_DIM)
    kc = k.reshape(bsz, nc, C, RET_HEADS, RET_KEY_DIM)
    vc = v.reshape(bsz, nc, C, RET_HEADS, RET_VAL_DIM)

    log_decay = jnp.log1p(-jnp.exp2(-5.0 - jnp.arange(RET_HEADS, dtype=jnp.float32)))
    idx = jnp.arange(C, dtype=jnp.float32)
    rel = idx[:, None] - idx[None, :]
    decay_intra = jnp.where(rel >= 0, jnp.exp(jnp.maximum(rel, 0.0)[None] * log_decay[:, None, None]), 0.0)
    xi = jnp.exp((idx + 1.0)[None, :] * log_decay[:, None]).T[None, :, :, None]
    zeta = jnp.exp((C - 1.0 - idx)[None, :] * log_decay[:, None]).T[None, :, :, None]
    decay_chunk = jnp.exp(C * log_decay)[None, :, None, None]

    s = jnp.einsum('bnihd,bnjhd->bnhij', qc, kc) * decay_intra
    o_intra = jnp.einsum('bnhij,bnjhe->bnihe', s, vc)

    def step(state, inp):
        q_i, k_i, v_i = inp
        o = jnp.einsum('bihd,bhde->bihe', q_i, state) * xi
        state = decay_chunk * state + jnp.einsum('bjhd,bjhe->bhde', k_i * zeta, v_i)
        return state, o

    state0 = jnp.zeros((bsz, RET_HEADS, RET_KEY_DIM, RET_VAL_DIM), jnp.float32)
    _, o_inter = lax.scan(step, state0, (jnp.moveaxis(qc, 1, 0), jnp.moveaxis(kc, 1, 0), jnp.moveaxis(vc, 1, 0)))
    o = (o_intra + jnp.moveaxis(o_inter, 0, 1)).astype(jnp.float32).reshape(bsz, seq, RET_HEADS, RET_VAL_DIM)

    mu = jnp.mean(o, axis=-1, keepdims=True)
    var = jnp.mean(jnp.square(o - mu), axis=-1, keepdims=True)
    o = ((o - mu) * lax.rsqrt(var + GN_EPS)).reshape(bsz, seq, RET_V_WIDTH) * gn_gain.astype(jnp.float32)
    return (jax.nn.silu(g) * o.astype(u.dtype)) @ w_o


def _moe(u, router_w, router_b, w_gu, b_gu, w_down, b_down):
    bsz, seq, d = u.shape
    h = u.reshape(bsz * seq, d)
    logits = (h @ router_w).astype(jnp.float32) + router_b.astype(jnp.float32)
    top_v, top_i = lax.top_k(logits, TOP_K)
    top_w = jax.nn.softmax(top_v, axis=-1)
    combine = jnp.einsum('tk,tke->te', top_w, jax.nn.one_hot(top_i, N_EXPERTS, dtype=jnp.float32)).astype(u.dtype)
    gu = jnp.einsum('td,edf->tef', h, w_gu) + b_gu
    gate = jnp.minimum(gu[..., :EXPERT_FF], SWIGLU_LIMIT)
    up = jnp.clip(gu[..., EXPERT_FF:], -SWIGLU_LIMIT, SWIGLU_LIMIT)
    act = (up + 1.0) * gate * jax.nn.sigmoid(SWIGLU_ALPHA * gate)
    y = jnp.einsum('tef,efd->td', act * combine[:, :, None], w_down) + combine @ b_down
    return y.reshape(bsz, seq, d)


def setup_inputs(seed: int = 0) -> dict:
    key = jax.random.key(seed)
    ks = jax.random.split(key, 24)
    f32 = jnp.float32

    def nrm(k, shape, scale):
        return jax.random.normal(k, shape, f32) * scale

    x = nrm(ks[0], (BATCH, SEQ, D_MODEL), 1.0)
    c = nrm(ks[1], (BATCH, D_MODEL), 1.0)
    offsets = jax.random.randint(ks[2], (BATCH, 1), 0, MAX_POS_OFFSET, dtype=jnp.int32)
    positions = (jnp.arange(SEQ, dtype=jnp.int32)[None, :] + offsets).astype(jnp.int32)

    mod_w = nrm(ks[3], (D_MODEL, N_MOD * D_MODEL), ADALN_INIT_SCALE * D_MODEL ** -0.5)
    mod_b = nrm(ks[4], (N_MOD * D_MODEL,), 0.01)
    mod_layer = nrm(ks[5], (DEPTH, N_MOD, D_MODEL), 0.02)
    ln_gain = 1.0 + nrm(ks[6], (DEPTH, 2, D_MODEL), 0.02)
    ln_bias = nrm(ks[7], (DEPTH, 2, D_MODEL), 0.01)

    attn_col = jnp.concatenate([jnp.ones((Q_DIM + KV_DIM,), f32), jnp.full((KV_DIM,), DEEPNORM_BETA, f32)]) * D_MODEL ** -0.5
    attn_w_qkv = jax.random.normal(ks[8], (N_ATTN_LAYERS, D_MODEL, ATTN_QKV_DIM), f32) * attn_col
    attn_b_qkv = nrm(ks[9], (N_ATTN_LAYERS, ATTN_QKV_DIM), 0.01)
    attn_sinks = nrm(ks[10], (N_ATTN_LAYERS, ATTN_Q_HEADS), 1.0)
    attn_w_o = nrm(ks[11], (N_ATTN_LAYERS, Q_DIM, D_MODEL), DEEPNORM_BETA * Q_DIM ** -0.5)
    attn_b_o = nrm(ks[12], (N_ATTN_LAYERS, D_MODEL), 0.01)

    ret_col = jnp.concatenate([jnp.ones((2 * D_MODEL,), f32), jnp.full((RET_V_WIDTH,), DEEPNORM_BETA, f32),
                               jnp.ones((RET_V_WIDTH,), f32)]) * D_MODEL ** -0.5
    ret_w_qkvg = jax.random.normal(ks[13], (N_RET_LAYERS, D_MODEL, RET_PROJ_DIM), f32) * ret_col
    ret_gn_gain = 1.0 + nrm(ks[14], (N_RET_LAYERS, RET_V_WIDTH), 0.02)
    ret_w_o = nrm(ks[15], (N_RET_LAYERS, RET_V_WIDTH, D_MODEL), DEEPNORM_BETA * RET_V_WIDTH ** -0.5)

    router_w = nrm(ks[16], (DEPTH, D_MODEL, N_EXPERTS), D_MODEL ** -0.5)
    router_b = nrm(ks[17], (DEPTH, N_EXPERTS), 0.01)
    expert_w_gu = nrm(ks[18], (DEPTH, N_EXPERTS, D_MODEL, 2 * EXPERT_FF), DEEPNORM_BETA * D_MODEL ** -0.5)
    expert_b_gu = nrm(ks[19], (DEPTH, N_EXPERTS, 2 * EXPERT_FF), 0.01)
    expert_w_down = nrm(ks[20], (DEPTH, N_EXPERTS, EXPERT_FF, D_MODEL), DEEPNORM_BETA * EXPERT_FF ** -0.5)
    expert_b_down = nrm(ks[21], (DEPTH, N_EXPERTS, D_MODEL), 0.01)

    return {'x': x, 'c': c, 'positions': positions, 'mod_w': mod_w, 'mod_b': mod_b, 'mod_layer': mod_layer,
            'ln_gain': ln_gain, 'ln_bias': ln_bias, 'attn_w_qkv': attn_w_qkv, 'attn_b_qkv': attn_b_qkv,
            'attn_sinks': attn_sinks, 'attn_w_o': attn_w_o, 'attn_b_o': attn_b_o, 'ret_w_qkvg': ret_w_qkvg,
            'ret_gn_gain': ret_gn_gain, 'ret_w_o': ret_w_o, 'router_w': router_w, 'router_b': router_b,
            'expert_w_gu': expert_w_gu, 'expert_b_gu': expert_b_gu, 'expert_w_down': expert_w_down,
            'expert_b_down': expert_b_down}


def reference(x, c, positions, mod_w, mod_b, mod_layer, ln_gain, ln_bias, attn_w_qkv, attn_b_qkv, attn_sinks,
              attn_w_o, attn_b_o, ret_w_qkvg, ret_gn_gain, ret_w_o, router_w, router_b, expert_w_gu, expert_b_gu,
              expert_w_down, expert_b_down):
    bsz = x.shape[0]
    mod = (jax.nn.silu(c) @ mod_w + mod_b).reshape(bsz, N_MOD, D_MODEL)

    pos = positions.astype(jnp.float32)[..., None]
    inv_a = ROPE_THETA ** (-jnp.arange(0, ROPE_DIM, 2, dtype=jnp.float32) / ROPE_DIM)
    ang_a = pos * inv_a
    cos_a = jnp.cos(ang_a)[:, :, None, :].astype(x.dtype)
    sin_a = jnp.sin(ang_a)[:, :, None, :].astype(x.dtype)
    inv_r = RET_ROT_BASE ** (-jnp.linspace(0.0, 1.0, RET_KEY_DIM // 2, dtype=jnp.float32))
    ang_r = pos * inv_r
    cos_r = jnp.cos(ang_r)[:, :, None, :].astype(x.dtype)
    sin_r = jnp.sin(ang_r)[:, :, None, :].astype(x.dtype)

    for i in range(DEPTH):
        m = mod + mod_layer[i]
        shift_t, scale_t, gate_t, shift_f, scale_f, gate_f = [m[:, j, None, :] for j in range(N_MOD)]
        j = i // N_MIXERS

        u = x * (1.0 + scale_t) + shift_t
        if i % N_MIXERS == 0:
            y = _sliding_window_sink_attention(u, attn_w_qkv[j], attn_b_qkv[j], attn_sinks[j], attn_w_o[j],
                                               attn_b_o[j], cos_a, sin_a)
        else:
            y = _retention(u, ret_w_qkvg[j], ret_gn_gain[j], ret_w_o[j], cos_r, sin_r)
        x = _layer_norm(DEEPNORM_ALPHA * x + (1.0 + gate_t) * y, ln_gain[i, 0], ln_bias[i, 0])

        u = x * (1.0 + scale_f) + shift_f
        y = _moe(u, router_w[i], router_b[i], expert_w_gu[i], expert_b_gu[i], expert_w_down[i], expert_b_down[i])
        x = _layer_norm(DEEPNORM_ALPHA * x + (1.0 + gate_f) * y, ln_gain[i, 1], ln_bias[i, 1])
    return x
```

```python
import functools

import jax
import jax.numpy as jnp
from jax import lax
from jax.experimental import pallas as pl
from jax.experimental.pallas import tpu as pltpu

HEAD_DIM = 64
KV_HEADS = 8
ROPE_DIM = 16
ROPE_THETA = 500000.0
WINDOW = 128
RET_HEADS = 16
RET_VALUE_FACTOR = 2
RET_CHUNK = 128
RET_ROT_BASE = 10000.0
GN_EPS = 1e-6
TOP_K = 4
SWIGLU_LIMIT = 7.0
SWIGLU_ALPHA = 1.702
LN_EPS = 1e-5
N_MOD = 6

LANES = 128
VMEM_LIMIT_BYTES = 56 * 1024 * 1024
BM = 1024
BN = 512
CAST_ROWS = 512

F32 = jnp.float32
BF16 = jnp.bfloat16


def _params(semantics):
    return pltpu.CompilerParams(dimension_semantics=semantics, vmem_limit_bytes=VMEM_LIMIT_BYTES)


def _cast_weight(w_ref, wbf_ref):
    k = w_ref.shape[0]
    rows = min(CAST_ROWS, k)

    def body(i, carry):
        r = pl.multiple_of(i * rows, rows)
        wbf_ref[pl.ds(r, rows), :] = w_ref[pl.ds(r, rows), :].astype(BF16)
        return carry

    lax.fori_loop(0, k // rows, body, 0)


def _mod_kernel(c_ref, w_ref, b_ref, o_ref):
    c = c_ref[...]
    s = (c * jax.nn.sigmoid(c)).astype(BF16)
    acc = jnp.dot(s, w_ref[...].astype(BF16), preferred_element_type=F32)
    o_ref[...] = acc + b_ref[...]


def _mod_proj(c, mod_w, mod_b):
    bsz, d = c.shape
    n = mod_w.shape[1]
    rows = 8
    c_pad = jnp.zeros((rows, d), F32).at[:bsz].set(c)
    bn = min(1024, n)
    out = pl.pallas_call(
        _mod_kernel,
        grid=(n // bn,),
        in_specs=[pl.BlockSpec((rows, d), lambda j: (0, 0)),
                  pl.BlockSpec((d, bn), lambda j: (0, j)),
                  pl.BlockSpec((1, bn), lambda j: (0, j))],
        out_specs=pl.BlockSpec((rows, bn), lambda j: (0, j)),
        out_shape=jax.ShapeDtypeStruct((rows, n), F32),
        compiler_params=_params(("arbitrary",)),
        name="mod_proj",
    )(c_pad, mod_w, mod_b.reshape(1, n))
    return out[:bsz]


def _ln_mod_kernel(*refs, apply_ln, emit_u):
    it = iter(refs)
    z_ref = next(it)
    if apply_ln:
        g_ref, b_ref = next(it), next(it)
    if emit_u:
        scale_ref, shift_ref = next(it), next(it)
    if apply_ln:
        x_ref = next(it)
    if emit_u:
        u_ref = next(it)

    x = z_ref[...]
    if apply_ln:
        mu = jnp.mean(x, axis=-1, keepdims=True)
        xc = x - mu
        var = jnp.mean(xc * xc, axis=-1, keepdims=True)
        x = xc * lax.rsqrt(var + LN_EPS) * g_ref[0] + b_ref[0]
        x_ref[...] = x
    if emit_u:
        u_ref[...] = (x * (1.0 + scale_ref[0]) + shift_ref[0]).astype(BF16)


def _ln_mod(z, seq, ln=None, mod=None, tm=256):
    t, d = z.shape
    tm = min(tm, seq)
    apply_ln, emit_u = ln is not None, mod is not None
    args, in_specs = [z], [pl.BlockSpec((tm, d), lambda m: (m, 0))]
    if apply_ln:
        gain, bias, layer, slot = ln
        row = layer * gain.shape[1] + slot
        for a in (gain, bias):
            args.append(a.reshape(-1, 1, d))
            in_specs.append(pl.BlockSpec((1, 1, d), lambda m, row=row: (row, 0, 0)))
    if emit_u:
        for a in mod:
            args.append(a)
            in_specs.append(pl.BlockSpec((1, 1, d), lambda m: ((m * tm) // seq, 0, 0)))
    out_shape, out_specs = [], []
    if apply_ln:
        out_shape.append(jax.ShapeDtypeStruct((t, d), F32))
        out_specs.append(pl.BlockSpec((tm, d), lambda m: (m, 0)))
    if emit_u:
        out_shape.append(jax.ShapeDtypeStruct((t, d), BF16))
        out_specs.append(pl.BlockSpec((tm, d), lambda m: (m, 0)))
    return pl.pallas_call(
        functools.partial(_ln_mod_kernel, apply_ln=apply_ln, emit_u=emit_u),
        grid=(t // tm,),
        in_specs=in_specs,
        out_specs=out_specs,
        out_shape=out_shape,
        compiler_params=_params(("arbitrary",)),
        name="ln_mod",
    )(*args)


def _rope_half(blk, c, s_hi, s_lo):
    half = ROPE_DIM // 2
    return blk * c + pltpu.roll(blk, LANES - half, 1) * s_hi + pltpu.roll(blk, half, 1) * s_lo


def _attn_qkv_kernel(u_ref, w_ref, b_ref, c_ref, shi_ref, slo_ref, o_ref, wbf_ref, *, rope_tiles):
    j = pl.program_id(0)

    @pl.when(pl.program_id(1) == 0)
    def _():
        _cast_weight(w_ref.at[0], wbf_ref)

    acc = jnp.dot(u_ref[...], wbf_ref[...], preferred_element_type=F32) + b_ref[0]
    groups = acc.shape[1] // LANES

    @pl.when(j < rope_tiles)
    def _():
        c, s_hi, s_lo = c_ref[...], shi_ref[...], slo_ref[...]
        for g in range(groups):
            sl = slice(g * LANES, (g + 1) * LANES)
            o_ref[:, sl] = _rope_half(acc[:, sl], c, s_hi, s_lo).astype(BF16)

    @pl.when(j >= rope_tiles)
    def _():
        o_ref[...] = acc.astype(BF16)


def _attn_qkv(u, w, b, layer, tabs, rope_cols):
    t, d = u.shape
    n = w.shape[2]
    bm, bn = min(BM, t), min(BN, n)
    assert rope_cols % bn == 0 and n % bn == 0 and t % bm == 0
    tab_spec = pl.BlockSpec((bm, LANES), lambda j, m: (m, 0))
    return pl.pallas_call(
        functools.partial(_attn_qkv_kernel, rope_tiles=rope_cols // bn),
        grid=(n // bn, t // bm),
        in_specs=[pl.BlockSpec((bm, d), lambda j, m: (m, 0)),
                  pl.BlockSpec((1, d, bn), lambda j, m: (layer, 0, j)),
                  pl.BlockSpec((1, 1, bn), lambda j, m: (layer, 0, j)),
                  tab_spec, tab_spec, tab_spec],
        out_specs=pl.BlockSpec((bm, bn), lambda j, m: (m, j)),
        out_shape=jax.ShapeDtypeStruct((t, n), BF16),
        scratch_shapes=[pltpu.VMEM((d, bn), BF16)],
        compiler_params=_params(("arbitrary", "arbitrary")),
        name="attn_qkv",
    )(u, w, b.reshape(b.shape[0], 1, n), *tabs)


def _ret_proj_kernel(u_ref, w_ref, c_ref, s_ref, o_ref, wbf_ref, *, q_tiles, k_tiles, v_tiles, k_scale):
    j = pl.program_id(0)

    @pl.when(pl.program_id(1) == 0)
    def _():
        _cast_weight(w_ref.at[0], wbf_ref)

    acc = jnp.dot(u_ref[...], wbf_ref[...], preferred_element_type=F32)
    bn = acc.shape[1]
    groups = bn // LANES
    tab_groups = c_ref.shape[1] // LANES

    def rotate(scale):
        even = (lax.broadcasted_iota(jnp.int32, (1, LANES), 1) % 2) == 0
        for g in range(groups):
            sl = slice(g * LANES, (g + 1) * LANES)
            tl = slice((g % tab_groups) * LANES, (g % tab_groups + 1) * LANES)
            blk = acc[:, sl]
            partner = jnp.where(even, pltpu.roll(blk, LANES - 1, 1), pltpu.roll(blk, 1, 1))
            out = blk * c_ref[:, tl] + partner * s_ref[:, tl]
            if scale != 1.0:
                out = out * scale
            o_ref[:, sl] = out.astype(BF16)

    @pl.when(j < q_tiles)
    def _():
        rotate(1.0)

    @pl.when((j >= q_tiles) & (j < q_tiles + k_tiles))
    def _():
        rotate(k_scale)

    @pl.when((j >= q_tiles + k_tiles) & (j < q_tiles + k_tiles + v_tiles))
    def _():
        o_ref[...] = acc.astype(BF16)

    @pl.when(j >= q_tiles + k_tiles + v_tiles)
    def _():
        o_ref[...] = (acc * jax.nn.sigmoid(acc)).astype(BF16)


def _ret_proj(u, w, layer, tabs, d_model, v_width, key_dim):
    t, d = u.shape
    n = w.shape[2]
    bm, bn = min(BM, t), min(BN, n)
    tab_w = tabs[0].shape[1]
    assert d_model % bn == 0 and v_width % bn == 0 and bn % tab_w == 0 and t % bm == 0
    tab_spec = pl.BlockSpec((bm, tab_w), lambda j, m: (m, 0))
    return pl.pallas_call(
        functools.partial(_ret_proj_kernel, q_tiles=d_model // bn, k_tiles=d_model // bn, v_tiles=v_width // bn,
                          k_scale=key_dim ** -0.5),
        grid=(n // bn, t // bm),
        in_specs=[pl.BlockSpec((bm, d), lambda j, m: (m, 0)),
                  pl.BlockSpec((1, d, bn), lambda j, m: (layer, 0, j)),
                  tab_spec, tab_spec],
        out_specs=pl.BlockSpec((bm, bn), lambda j, m: (m, j)),
        out_shape=jax.ShapeDtypeStruct((t, n), BF16),
        scratch_shapes=[pltpu.VMEM((d, bn), BF16)],
        compiler_params=_params(("arbitrary", "arbitrary")),
        name="ret_proj",
    )(u, w, *tabs)


def _residual_kernel(*refs, alpha, has_bias, has_comb):
    it = iter(refs)
    a_ref, w_ref = next(it), next(it)
    b_ref = next(it) if has_bias else None
    x_ref, gate_ref = next(it), next(it)
    if has_comb:
        comb_ref, bd_ref = next(it), next(it)
    o_ref, wbf_ref = next(it), next(it)

    @pl.when(pl.program_id(1) == 0)
    def _():
        _cast_weight(w_ref.at[0], wbf_ref)

    y = jnp.dot(a_ref[...], wbf_ref[...], preferred_element_type=F32)
    if has_bias:
        y = y + b_ref[0]
    if has_comb:
        y = y + jnp.dot(comb_ref[...], bd_ref[0], preferred_element_type=F32)
    o_ref[...] = alpha * x_ref[...] + (1.0 + gate_ref[0]) * y


def _residual_mm(a, w, layer, x, gate, seq, alpha, bias=None, comb=None):
    t, k = a.shape
    d = w.shape[2]
    bn = min(BN, d)
    bm = min(BM if k * BM * 2 <= 8 * 1024 * 1024 else BM // 2, seq)
    args = [a, w]
    in_specs = [pl.BlockSpec((bm, k), lambda j, m: (m, 0)),
                pl.BlockSpec((1, k, bn), lambda j, m: (layer, 0, j), pipeline_mode=pl.Buffered(1))]
    if bias is not None:
        args.append(bias.reshape(bias.shape[0], 1, d))
        in_specs.append(pl.BlockSpec((1, 1, bn), lambda j, m: (layer, 0, j)))
    args += [x, gate]
    in_specs += [pl.BlockSpec((bm, bn), lambda j, m: (m, j)),
                 pl.BlockSpec((1, 1, bn), lambda j, m: ((m * bm) // seq, 0, j))]
    if comb is not None:
        args += list(comb)
        in_specs += [pl.BlockSpec((bm, LANES), lambda j, m: (m, 0)),
                     pl.BlockSpec((1, LANES, bn), lambda j, m: (layer, 0, j))]
    return pl.pallas_call(
        functools.partial(_residual_kernel, alpha=alpha, has_bias=bias is not None, has_comb=comb is not None),
        grid=(d // bn, t // bm),
        in_specs=in_specs,
        out_specs=pl.BlockSpec((bm, bn), lambda j, m: (m, j)),
        out_shape=jax.ShapeDtypeStruct((t, d), F32),
        scratch_shapes=[pltpu.VMEM((k, bn), BF16)],
        compiler_params=_params(("arbitrary", "arbitrary")),
        name="residual_mm",
    )(*args)


def _attn_kernel(sink_ref, q_ref, kp_ref, kc_ref, vp_ref, vc_ref, o_ref, *, group, blocks_per_seq, heads_per_step):
    n = pl.program_id(0)
    p = pl.program_id(1)
    has_prev = (n % blocks_per_seq) != 0
    qi = lax.broadcasted_iota(jnp.int32, (WINDOW, 2 * WINDOW), 0)
    kj = lax.broadcasted_iota(jnp.int32, (WINDOW, 2 * WINDOW), 1)
    diff = qi + WINDOW - kj
    visible = (diff >= 0) & (diff < WINDOW) & (has_prev | (kj >= WINDOW))
    scale = HEAD_DIM ** -0.5
    for kk in range(heads_per_step):
        ks = slice(kk * HEAD_DIM, (kk + 1) * HEAD_DIM)
        kcat = jnp.concatenate([kp_ref[:, ks], kc_ref[:, ks]], axis=0)
        vcat = jnp.concatenate([vp_ref[:, ks], vc_ref[:, ks]], axis=0)
        for g in range(group):
            h = kk * group + g
            hs = slice(h * HEAD_DIM, (h + 1) * HEAD_DIM)
            s = lax.dot_general(q_ref[:, hs], kcat, (((1,), (1,)), ((), ())), preferred_element_type=F32) * scale
            s = jnp.where(visible, s, -jnp.inf)
            sink = sink_ref[p * heads_per_step * group + h]
            mx = jnp.maximum(jnp.max(s, axis=-1, keepdims=True), sink)
            e = jnp.exp(s - mx)
            denom = jnp.sum(e, axis=-1, keepdims=True) + jnp.exp(sink - mx)
            prob = (e * (1.0 / denom)).astype(BF16)
            o_ref[:, hs] = jnp.dot(prob, vcat, preferred_element_type=F32).astype(BF16)


def _attention(qkv, sinks, seq, q_dim, kv_dim):
    t = qkv.shape[0]
    kv_heads = kv_dim // HEAD_DIM
    group = (q_dim // HEAD_DIM) // kv_heads
    hps = LANES // HEAD_DIM
    qw = hps * group * HEAD_DIM
    k_blk0 = q_dim // LANES
    v_blk0 = (q_dim + kv_dim) // LANES
    kv_spec = lambda off, prev: pl.BlockSpec(
        (WINDOW, LANES), (lambda n, p: (jnp.maximum(n - 1, 0), off + p)) if prev else (lambda n, p: (n, off + p)))
    return pl.pallas_call(
        functools.partial(_attn_kernel, group=group, blocks_per_seq=seq // WINDOW, heads_per_step=hps),
        grid=(t // WINDOW, kv_heads // hps),
        in_specs=[pl.BlockSpec(memory_space=pltpu.SMEM),
                  pl.BlockSpec((WINDOW, qw), lambda n, p: (n, p)),
                  kv_spec(k_blk0, True), kv_spec(k_blk0, False), kv_spec(v_blk0, True), kv_spec(v_blk0, False)],
        out_specs=pl.BlockSpec((WINDOW, qw), lambda n, p: (n, p)),
        out_shape=jax.ShapeDtypeStruct((t, q_dim), BF16),
        compiler_params=_params(("arbitrary", "arbitrary")),
        name="swa_attention",
    )(sinks, qkv, qkv, qkv, qkv, qkv)


def _ret_kernel(q_ref, k_ref, v_ref, g_ref, dec_ref, xi_ref, zeta_ref, dc_ref, gain_ref, o_ref, state_ref):
    c = RET_CHUNK
    dv = v_ref.shape[1]
    state_ref[...] = jnp.zeros_like(state_ref)
    rep = dv // LANES
    xi = jnp.concatenate([xi_ref[0]] * rep, axis=1)
    zeta = jnp.concatenate([zeta_ref[0]] * (k_ref.shape[1] // LANES), axis=1)
    dc = jnp.concatenate([dc_ref[0]] * rep, axis=1)[:1]

    def body(i, carry):
        r = pl.multiple_of(i * c, c)
        q = q_ref[pl.ds(r, c), :]
        k = k_ref[pl.ds(r, c), :]
        v = v_ref[pl.ds(r, c), :]
        s = lax.dot_general(q, k, (((1,), (1,)), ((), ())), preferred_element_type=F32) * dec_ref[0]
        state = state_ref[...]
        o = jnp.dot(s.astype(BF16), v, preferred_element_type=F32)
        o = o + jnp.dot(q, state.astype(BF16), preferred_element_type=F32) * xi
        kz = (k.astype(F32) * zeta).astype(BF16)
        state_ref[...] = dc * state + lax.dot_general(kz, v, (((0,), (0,)), ((), ())), preferred_element_type=F32)
        mu = jnp.mean(o, axis=-1, keepdims=True)
        oc = o - mu
        var = jnp.mean(oc * oc, axis=-1, keepdims=True)
        on = oc * lax.rsqrt(var + GN_EPS) * gain_ref[0]
        o_ref[pl.ds(r, c), :] = (g_ref[pl.ds(r, c), :].astype(F32) * on).astype(BF16)
        return carry

    lax.fori_loop(0, q_ref.shape[0] // c, body, 0)


def _retention(proj, gn_gain, layer, consts, seq, d_model, v_width):
    t = proj.shape[0]
    heads = consts[0].shape[0]
    dk, dv = d_model // heads, v_width // heads
    k0, v0, g0 = d_model // dk, 2 * d_model // dv, (2 * d_model + v_width) // dv
    cspec = pl.BlockSpec((1, RET_CHUNK, LANES), lambda b, h: (h, 0, 0))
    return pl.pallas_call(
        _ret_kernel,
        grid=(t // seq, heads),
        in_specs=[pl.BlockSpec((seq, dk), lambda b, h: (b, h)),
                  pl.BlockSpec((seq, dk), lambda b, h: (b, k0 + h)),
                  pl.BlockSpec((seq, dv), lambda b, h: (b, v0 + h)),
                  pl.BlockSpec((seq, dv), lambda b, h: (b, g0 + h)),
                  cspec, cspec, cspec, cspec,
                  pl.BlockSpec((1, 1, dv), lambda b, h: (layer, 0, h))],
        out_specs=pl.BlockSpec((seq, dv), lambda b, h: (b, h)),
        out_shape=jax.ShapeDtypeStruct((t, v_width), BF16),
        scratch_shapes=[pltpu.VMEM((dk, dv), F32)],
        compiler_params=_params(("arbitrary", "arbitrary")),
        name="retention",
    )(proj, proj, proj, proj, *consts, gn_gain.reshape(gn_gain.shape[0], 1, v_width))


def _router_kernel(u_ref, w_ref, b_ref, comb_ref, *, n_experts):
    logits = jnp.dot(u_ref[...], w_ref[0].astype(BF16), preferred_element_type=F32) + b_ref[0]
    lane = lax.broadcasted_iota(jnp.int32, logits.shape, 1)
    work = jnp.where(lane < n_experts, logits, -jnp.inf)
    vals, hots = [], []
    for _ in range(TOP_K):
        mx = jnp.max(work, axis=-1, keepdims=True)
        idx = jnp.min(jnp.where(work == mx, lane, LANES), axis=-1, keepdims=True)
        hot = lane == idx
        work = jnp.where(hot, -jnp.inf, work)
        vals.append(mx)
        hots.append(hot)
    es = [jnp.exp(v - vals[0]) for v in vals]
    inv = 1.0 / functools.reduce(lambda a, b: a + b, es)
    comb = jnp.zeros(logits.shape, F32)
    for e, hot in zip(es, hots):
        comb = jnp.where(hot, e * inv, comb)
    comb_ref[...] = comb


def _router(u, rw_pad, rb_pad, layer, n_experts):
    t, d = u.shape
    bm = min(BM, t)
    return pl.pallas_call(
        functools.partial(_router_kernel, n_experts=n_experts),
        grid=(t // bm,),
        in_specs=[pl.BlockSpec((bm, d), lambda m: (m, 0)),
                  pl.BlockSpec((1, d, LANES), lambda m: (layer, 0, 0)),
                  pl.BlockSpec((1, 1, LANES), lambda m: (layer, 0, 0))],
        out_specs=pl.BlockSpec((bm, LANES), lambda m: (m, 0)),
        out_shape=jax.ShapeDtypeStruct((t, LANES), F32),
        compiler_params=_params(("arbitrary",)),
        name="router",
    )(u, rw_pad, rb_pad)


def _moe_gu_kernel(u_ref, w_ref, b_ref, comb_ref, o_ref, wbf_ref):
    e = pl.program_id(0)

    @pl.when(pl.program_id(1) == 0)
    def _():
        _cast_weight(w_ref.at[0, 0], wbf_ref)

    h = jnp.dot(u_ref[...], wbf_ref[...], preferred_element_type=F32) + b_ref[0, 0]
    ff = h.shape[1] // 2
    gate = jnp.minimum(h[:, :ff], SWIGLU_LIMIT)
    up = jnp.clip(h[:, ff:], -SWIGLU_LIMIT, SWIGLU_LIMIT)
    act = (up + 1.0) * gate * jax.nn.sigmoid(SWIGLU_ALPHA * gate)
    comb = comb_ref[...]
    lane = lax.broadcasted_iota(jnp.int32, comb.shape, 1)
    col = jnp.sum(jnp.where(lane == e, comb, 0.0), axis=-1, keepdims=True)
    o_ref[...] = (act * col).astype(BF16)


def _moe_gu(u, w_gu, b_gu, comb, layer):
    t, d = u.shape
    n_exp, ff2 = w_gu.shape[1], w_gu.shape[3]
    bm = min(BM, t)
    return pl.pallas_call(
        _moe_gu_kernel,
        grid=(n_exp, t // bm),
        in_specs=[pl.BlockSpec((bm, d), lambda e, m: (m, 0)),
                  pl.BlockSpec((1, 1, d, ff2), lambda e, m: (layer, e, 0, 0)),
                  pl.BlockSpec((1, 1, 1, ff2), lambda e, m: (layer, e, 0, 0)),
                  pl.BlockSpec((bm, LANES), lambda e, m: (m, 0))],
        out_specs=pl.BlockSpec((bm, ff2 // 2), lambda e, m: (m, e)),
        out_shape=jax.ShapeDtypeStruct((t, n_exp * ff2 // 2), BF16),
        scratch_shapes=[pltpu.VMEM((d, ff2), BF16)],
        compiler_params=_params(("arbitrary", "arbitrary")),
        name="moe_gate_up",
    )(u, w_gu, b_gu.reshape(b_gu.shape[0], n_exp, 1, ff2), comb)


def _attn_rope_tables(pos):
    half = ROPE_DIM // 2
    inv = ROPE_THETA ** (-jnp.arange(0, ROPE_DIM, 2, dtype=F32) / ROPE_DIM)
    ang = pos[:, None] * inv
    cos, sin = jnp.cos(ang), jnp.sin(ang)
    t = pos.shape[0]
    ones = jnp.ones((t, HEAD_DIM - ROPE_DIM), F32)
    zeros_h = jnp.zeros((t, half), F32)
    zeros_r = jnp.zeros((t, HEAD_DIM - ROPE_DIM), F32)
    c = jnp.concatenate([cos, cos, ones], axis=1)
    s_hi = jnp.concatenate([-sin, zeros_h, zeros_r], axis=1)
    s_lo = jnp.concatenate([zeros_h, sin, zeros_r], axis=1)
    rep = LANES // HEAD_DIM
    return tuple(jnp.tile(a, (1, rep)) for a in (c, s_hi, s_lo))


def _ret_rot_tables(pos, key_dim):
    inv = RET_ROT_BASE ** (-jnp.linspace(0.0, 1.0, key_dim // 2, dtype=F32))
    ang = pos[:, None] * inv
    cos, sin = jnp.cos(ang), jnp.sin(ang)
    c = jnp.repeat(cos, 2, axis=1)
    s = jnp.stack([-sin, sin], axis=-1).reshape(pos.shape[0], key_dim)
    return c, s


def _ret_decay_tables(heads):
    c = RET_CHUNK
    log_decay = jnp.log1p(-jnp.exp2(-5.0 - jnp.arange(heads, dtype=F32)))
    idx = jnp.arange(c, dtype=F32)
    rel = idx[:, None] - idx[None, :]
    decay_intra = jnp.where(rel >= 0, jnp.exp(jnp.maximum(rel, 0.0)[None] * log_decay[:, None, None]), 0.0)
    xi = jnp.exp((idx + 1.0)[None, :] * log_decay[:, None])
    zeta = jnp.exp((c - 1.0 - idx)[None, :] * log_decay[:, None])
    decay_chunk = jnp.exp(c * log_decay)
    lanes = lambda a: jnp.broadcast_to(a[:, :, None], (heads, c, LANES))
    return (decay_intra, lanes(xi), lanes(zeta), jnp.broadcast_to(decay_chunk[:, None, None], (heads, c, LANES)))


def kernel(x, c, positions, mod_w, mod_b, mod_layer, ln_gain, ln_bias, attn_w_qkv, attn_b_qkv, attn_sinks, attn_w_o,
           attn_b_o, ret_w_qkvg, ret_gn_gain, ret_w_o, router_w, router_b, expert_w_gu, expert_b_gu, expert_w_down,
           expert_b_down):
    bsz, seq, d = x.shape
    t = bsz * seq
    depth = mod_layer.shape[0]
    alpha = (2 * depth) ** 0.25
    n_exp = router_w.shape[2]
    ff = expert_w_down.shape[2]
    q_dim = attn_w_o.shape[1]
    kv_dim = (attn_w_qkv.shape[2] - q_dim) // 2
    v_width = ret_w_o.shape[1]
    key_dim = d // RET_HEADS
    assert n_exp <= LANES and seq % WINDOW == 0 and seq % RET_CHUNK == 0

    mod = _mod_proj(c, mod_w, mod_b).reshape(bsz, N_MOD, d)
    m_all = mod[None] + mod_layer[:, None]
    mrow = lambda i, j: m_all[i, :, j][:, None, :]

    pos = positions.astype(F32).reshape(t)
    attn_tabs = _attn_rope_tables(pos)
    ret_tabs = _ret_rot_tables(pos, key_dim)
    ret_consts = _ret_decay_tables(RET_HEADS)

    rw_pad = jnp.zeros((depth, d, LANES), F32).at[:, :, :n_exp].set(router_w)
    rb_pad = jnp.zeros((depth, 1, LANES), F32).at[:, 0, :n_exp].set(router_b)
    bd_pad = jnp.zeros((depth, LANES, d), F32).at[:, :n_exp].set(expert_b_down)
    w_down_flat = expert_w_down.reshape(depth, n_exp * ff, d)

    xcur = x.reshape(t, d)
    (u,) = _ln_mod(xcur, seq, mod=(mrow(0, 1), mrow(0, 0)))
    for i in range(depth):
        j = i // 2
        if i % 2 == 0:
            qkv = _attn_qkv(u, attn_w_qkv, attn_b_qkv, j, attn_tabs, q_dim + kv_dim)
            o = _attention(qkv, attn_sinks[j], seq, q_dim, kv_dim)
            z = _residual_mm(o, attn_w_o, j, xcur, mrow(i, 2), seq, alpha, bias=attn_b_o)
        else:
            proj = _ret_proj(u, ret_w_qkvg, j, ret_tabs, d, v_width, key_dim)
            o = _retention(proj, ret_gn_gain, j, ret_consts, seq, d, v_width)
            z = _residual_mm(o, ret_w_o, j, xcur, mrow(i, 2), seq, alpha)
        xcur, u = _ln_mod(z, seq, ln=(ln_gain, ln_bias, i, 0), mod=(mrow(i, 4), mrow(i, 3)))

        comb = _router(u, rw_pad, rb_pad, i, n_exp)
        act = _moe_gu(u, expert_w_gu, expert_b_gu, comb, i)
        z = _residual_mm(act, w_down_flat, i, xcur, mrow(i, 5), seq, alpha, comb=(comb, bd_pad))
        if i + 1 < depth:
            xcur, u = _ln_mod(z, seq, ln=(ln_gain, ln_bias, i, 1), mod=(mrow(i + 1, 1), mrow(i + 1, 0)))
        else:
            (xcur,) = _ln_mod(z, seq, ln=(ln_gain, ln_bias, i, 1))
    return xcur.reshape(bsz, seq, d)
```

```python
import functools

import jax
import jax.numpy as jnp
from jax import lax
from jax.experimental import pallas as pl
from jax.experimental.pallas import tpu as pltpu

HEAD_DIM = 64
KV_HEADS = 8
ROPE_DIM = 16
ROPE_THETA = 500000.0
WINDOW = 128
RET_HEADS = 16
RET_VALUE_FACTOR = 2
RET_CHUNK = 128
RET_ROT_BASE = 10000.0
GN_EPS = 1e-6
TOP_K = 4
SWIGLU_LIMIT = 7.0
SWIGLU_ALPHA = 1.702
LN_EPS = 1e-5
N_MOD = 6

LANES = 128
VMEM_LIMIT_BYTES = 56 * 1024 * 1024
BM = 1024
BN = 512
CAST_ROWS = 512
MOE_TM = 256

F32 = jnp.float32
BF16 = jnp.bfloat16
I32 = jnp.int32


def _params(semantics, **kw):
    return pltpu.CompilerParams(dimension_semantics=semantics, vmem_limit_bytes=VMEM_LIMIT_BYTES, **kw)


def _cast_weight(w_ref, wbf_ref):
    k = w_ref.shape[0]
    rows = min(CAST_ROWS, k)

    def body(i, carry):
        r = pl.multiple_of(i * rows, rows)
        wbf_ref[pl.ds(r, rows), :] = w_ref[pl.ds(r, rows), :].astype(BF16)
        return carry

    lax.fori_loop(0, k // rows, body, 0)


def _mod_kernel(c_ref, w_ref, b_ref, o_ref):
    c = c_ref[...]
    s = (c * jax.nn.sigmoid(c)).astype(BF16)
    acc = jnp.dot(s, w_ref[...].astype(BF16), preferred_element_type=F32)
    o_ref[...] = acc + b_ref[...]


def _mod_proj(c, mod_w, mod_b):
    bsz, d = c.shape
    n = mod_w.shape[1]
    rows = 8
    c_pad = jnp.zeros((rows, d), F32).at[:bsz].set(c)
    bn = min(1024, n)
    out = pl.pallas_call(
        _mod_kernel,
        grid=(n // bn,),
        in_specs=[pl.BlockSpec((rows, d), lambda j: (0, 0)),
                  pl.BlockSpec((d, bn), lambda j: (0, j)),
                  pl.BlockSpec((1, bn), lambda j: (0, j))],
        out_specs=pl.BlockSpec((rows, bn), lambda j: (0, j)),
        out_shape=jax.ShapeDtypeStruct((rows, n), F32),
        compiler_params=_params(("arbitrary",)),
        name="mod_proj",
    )(c_pad, mod_w, mod_b.reshape(1, n))
    return out[:bsz]


def _ln_mod_kernel(*refs, apply_ln, emit_u, emit_u32, combine, alpha):
    it = iter(refs)
    if combine:
        y_refs = [next(it) for _ in range(TOP_K)]
        cw_ref, xres_ref, gate_ref = next(it), next(it), next(it)
    else:
        z_ref = next(it)
    if apply_ln:
        g_ref, b_ref = next(it), next(it)
    if emit_u:
        scale_ref, shift_ref = next(it), next(it)
    if apply_ln:
        x_ref = next(it)
    if emit_u:
        u_ref = next(it)
    if emit_u32:
        u32_ref = next(it)

    if combine:
        cw = cw_ref[...]
        y = y_refs[0][...] * cw[:, 0:1]
        for k in range(1, TOP_K):
            y = y + y_refs[k][...] * cw[:, k:k + 1]
        x = alpha * xres_ref[...] + (1.0 + gate_ref[0]) * y
    else:
        x = z_ref[...]
    if apply_ln:
        mu = jnp.mean(x, axis=-1, keepdims=True)
        xc = x - mu
        var = jnp.mean(xc * xc, axis=-1, keepdims=True)
        x = xc * lax.rsqrt(var + LN_EPS) * g_ref[0] + b_ref[0]
        x_ref[...] = x
    if emit_u:
        u = x * (1.0 + scale_ref[0]) + shift_ref[0]
        u_ref[...] = u.astype(BF16)
        if emit_u32:
            u32_ref[...] = u


def _ln_mod(z, seq, ln=None, mod=None, tm=256, emit_u32=False, combine=None):
    apply_ln, emit_u = ln is not None, mod is not None
    alpha = 1.0
    if combine is not None:
        ys, cw, xres, gate, alpha = combine
        t, d = xres.shape
        tm = min(tm // 2, seq)
        planes = t // tm
        args = [ys] * TOP_K + [cw, xres, gate]
        in_specs = [pl.BlockSpec((tm, d), lambda m, k=k: (k * planes + m, 0)) for k in range(TOP_K)]
        in_specs += [pl.BlockSpec((tm, LANES), lambda m: (m, 0)),
                     pl.BlockSpec((tm, d), lambda m: (m, 0)),
                     pl.BlockSpec((1, 1, d), lambda m: ((m * tm) // seq, 0, 0))]
    else:
        t, d = z.shape
        tm = min(tm, seq)
        args, in_specs = [z], [pl.BlockSpec((tm, d), lambda m: (m, 0))]
    if apply_ln:
        gain, bias, layer, slot = ln
        row = layer * gain.shape[1] + slot
        for a in (gain, bias):
            args.append(a.reshape(-1, 1, d))
            in_specs.append(pl.BlockSpec((1, 1, d), lambda m, row=row: (row, 0, 0)))
    if emit_u:
        for a in mod:
            args.append(a)
            in_specs.append(pl.BlockSpec((1, 1, d), lambda m: ((m * tm) // seq, 0, 0)))
    out_shape, out_specs = [], []
    for flag, dtype in ((apply_ln, F32), (emit_u, BF16), (emit_u32, F32)):
        if flag:
            out_shape.append(jax.ShapeDtypeStruct((t, d), dtype))
            out_specs.append(pl.BlockSpec((tm, d), lambda m: (m, 0)))
    return pl.pallas_call(
        functools.partial(_ln_mod_kernel, apply_ln=apply_ln, emit_u=emit_u, emit_u32=emit_u32,
                          combine=combine is not None, alpha=alpha),
        grid=(t // tm,),
        in_specs=in_specs,
        out_specs=out_specs,
        out_shape=out_shape,
        compiler_params=_params(("arbitrary",)),
        name="ln_mod",
    )(*args)


def _rope_half(blk, c, s_hi, s_lo):
    half = ROPE_DIM // 2
    return blk * c + pltpu.roll(blk, LANES - half, 1) * s_hi + pltpu.roll(blk, half, 1) * s_lo


def _attn_qkv_kernel(u_ref, w_ref, b_ref, c_ref, shi_ref, slo_ref, o_ref, wbf_ref, *, rope_tiles):
    j = pl.program_id(0)

    @pl.when(pl.program_id(1) == 0)
    def _():
        _cast_weight(w_ref.at[0], wbf_ref)

    acc = jnp.dot(u_ref[...], wbf_ref[...], preferred_element_type=F32) + b_ref[0]
    groups = acc.shape[1] // LANES

    @pl.when(j < rope_tiles)
    def _():
        c, s_hi, s_lo = c_ref[...], shi_ref[...], slo_ref[...]
        for g in range(groups):
            sl = slice(g * LANES, (g + 1) * LANES)
            o_ref[:, sl] = _rope_half(acc[:, sl], c, s_hi, s_lo).astype(BF16)

    @pl.when(j >= rope_tiles)
    def _():
        o_ref[...] = acc.astype(BF16)


def _attn_qkv(u, w, b, layer, tabs, rope_cols):
    t, d = u.shape
    n = w.shape[2]
    bm, bn = min(BM, t), min(BN, n)
    assert rope_cols % bn == 0 and n % bn == 0 and t % bm == 0
    tab_spec = pl.BlockSpec((bm, LANES), lambda j, m: (m, 0))
    return pl.pallas_call(
        functools.partial(_attn_qkv_kernel, rope_tiles=rope_cols // bn),
        grid=(n // bn, t // bm),
        in_specs=[pl.BlockSpec((bm, d), lambda j, m: (m, 0)),
                  pl.BlockSpec((1, d, bn), lambda j, m: (layer, 0, j)),
                  pl.BlockSpec((1, 1, bn), lambda j, m: (layer, 0, j)),
                  tab_spec, tab_spec, tab_spec],
        out_specs=pl.BlockSpec((bm, bn), lambda j, m: (m, j)),
        out_shape=jax.ShapeDtypeStruct((t, n), BF16),
        scratch_shapes=[pltpu.VMEM((d, bn), BF16)],
        compiler_params=_params(("arbitrary", "arbitrary")),
        name="attn_qkv",
    )(u, w, b.reshape(b.shape[0], 1, n), *tabs)


def _ret_proj_kernel(u_ref, w_ref, c_ref, s_ref, o_ref, wbf_ref, *, q_tiles, k_tiles, v_tiles, k_scale):
    j = pl.program_id(0)

    @pl.when(pl.program_id(1) == 0)
    def _():
        _cast_weight(w_ref.at[0], wbf_ref)

    acc = jnp.dot(u_ref[...], wbf_ref[...], preferred_element_type=F32)
    bn = acc.shape[1]
    groups = bn // LANES
    tab_groups = c_ref.shape[1] // LANES

    def rotate(scale):
        even = (lax.broadcasted_iota(jnp.int32, (1, LANES), 1) % 2) == 0
        for g in range(groups):
            sl = slice(g * LANES, (g + 1) * LANES)
            tl = slice((g % tab_groups) * LANES, (g % tab_groups + 1) * LANES)
            blk = acc[:, sl]
            partner = jnp.where(even, pltpu.roll(blk, LANES - 1, 1), pltpu.roll(blk, 1, 1))
            out = blk * c_ref[:, tl] + partner * s_ref[:, tl]
            if scale != 1.0:
                out = out * scale
            o_ref[:, sl] = out.astype(BF16)

    @pl.when(j < q_tiles)
    def _():
        rotate(1.0)

    @pl.when((j >= q_tiles) & (j < q_tiles + k_tiles))
    def _():
        rotate(k_scale)

    @pl.when((j >= q_tiles + k_tiles) & (j < q_tiles + k_tiles + v_tiles))
    def _():
        o_ref[...] = acc.astype(BF16)

    @pl.when(j >= q_tiles + k_tiles + v_tiles)
    def _():
        o_ref[...] = (acc * jax.nn.sigmoid(acc)).astype(BF16)


def _ret_proj(u, w, layer, tabs, d_model, v_width, key_dim):
    t, d = u.shape
    n = w.shape[2]
    bm, bn = min(BM, t), min(BN, n)
    tab_w = tabs[0].shape[1]
    assert d_model % bn == 0 and v_width % bn == 0 and bn % tab_w == 0 and t % bm == 0
    tab_spec = pl.BlockSpec((bm, tab_w), lambda j, m: (m, 0))
    return pl.pallas_call(
        functools.partial(_ret_proj_kernel, q_tiles=d_model // bn, k_tiles=d_model // bn, v_tiles=v_width // bn,
                          k_scale=key_dim ** -0.5),
        grid=(n // bn, t // bm),
        in_specs=[pl.BlockSpec((bm, d), lambda j, m: (m, 0)),
                  pl.BlockSpec((1, d, bn), lambda j, m: (layer, 0, j)),
                  tab_spec, tab_spec],
        out_specs=pl.BlockSpec((bm, bn), lambda j, m: (m, j)),
        out_shape=jax.ShapeDtypeStruct((t, n), BF16),
        scratch_shapes=[pltpu.VMEM((d, bn), BF16)],
        compiler_params=_params(("arbitrary", "arbitrary")),
        name="ret_proj",
    )(u, w, *tabs)


def _residual_kernel(a_ref, w_ref, *refs, alpha, has_bias):
    it = iter(refs)
    b_ref = next(it) if has_bias else None
    x_ref, gate_ref, o_ref, wbf_ref = next(it), next(it), next(it), next(it)

    @pl.when(pl.program_id(1) == 0)
    def _():
        _cast_weight(w_ref.at[0], wbf_ref)

    y = jnp.dot(a_ref[...], wbf_ref[...], preferred_element_type=F32)
    if has_bias:
        y = y + b_ref[0]
    o_ref[...] = alpha * x_ref[...] + (1.0 + gate_ref[0]) * y


def _residual_mm(a, w, layer, x, gate, seq, alpha, bias=None):
    t, k = a.shape
    d = w.shape[2]
    bn = min(BN, d)
    bm = min(BM if k * BM * 2 <= 8 * 1024 * 1024 else BM // 2, seq)
    args = [a, w]
    in_specs = [pl.BlockSpec((bm, k), lambda j, m: (m, 0)),
                pl.BlockSpec((1, k, bn), lambda j, m: (layer, 0, j), pipeline_mode=pl.Buffered(1))]
    if bias is not None:
        args.append(bias.reshape(bias.shape[0], 1, d))
        in_specs.append(pl.BlockSpec((1, 1, bn), lambda j, m: (layer, 0, j)))
    args += [x, gate]
    in_specs += [pl.BlockSpec((bm, bn), lambda j, m: (m, j)),
                 pl.BlockSpec((1, 1, bn), lambda j, m: ((m * bm) // seq, 0, j))]
    return pl.pallas_call(
        functools.partial(_residual_kernel, alpha=alpha, has_bias=bias is not None),
        grid=(d // bn, t // bm),
        in_specs=in_specs,
        out_specs=pl.BlockSpec((bm, bn), lambda j, m: (m, j)),
        out_shape=jax.ShapeDtypeStruct((t, d), F32),
        scratch_shapes=[pltpu.VMEM((k, bn), BF16)],
        compiler_params=_params(("arbitrary", "arbitrary")),
        name="residual_mm",
    )(*args)


def _attn_kernel(sink_ref, q_ref, kp_ref, kc_ref, vp_ref, vc_ref, o_ref, *, group, blocks_per_seq, heads_per_step):
    n = pl.program_id(0)
    p = pl.program_id(1)
    has_prev = (n % blocks_per_seq) != 0
    qi = lax.broadcasted_iota(jnp.int32, (WINDOW, 2 * WINDOW), 0)
    kj = lax.broadcasted_iota(jnp.int32, (WINDOW, 2 * WINDOW), 1)
    diff = qi + WINDOW - kj
    visible = (diff >= 0) & (diff < WINDOW) & (has_prev | (kj >= WINDOW))
    scale = HEAD_DIM ** -0.5
    for kk in range(heads_per_step):
        ks = slice(kk * HEAD_DIM, (kk + 1) * HEAD_DIM)
        kcat = jnp.concatenate([kp_ref[:, ks], kc_ref[:, ks]], axis=0)
        vcat = jnp.concatenate([vp_ref[:, ks], vc_ref[:, ks]], axis=0)
        for g in range(group):
            h = kk * group + g
            hs = slice(h * HEAD_DIM, (h + 1) * HEAD_DIM)
            s = lax.dot_general(q_ref[:, hs], kcat, (((1,), (1,)), ((), ())), preferred_element_type=F32) * scale
            s = jnp.where(visible, s, -jnp.inf)
            sink = sink_ref[p * heads_per_step * group + h]
            mx = jnp.maximum(jnp.max(s, axis=-1, keepdims=True), sink)
            e = jnp.exp(s - mx)
            denom = jnp.sum(e, axis=-1, keepdims=True) + jnp.exp(sink - mx)
            prob = (e * (1.0 / denom)).astype(BF16)
            o_ref[:, hs] = jnp.dot(prob, vcat, preferred_element_type=F32).astype(BF16)


def _attention(qkv, sinks, seq, q_dim, kv_dim):
    t = qkv.shape[0]
    kv_heads = kv_dim // HEAD_DIM
    group = (q_dim // HEAD_DIM) // kv_heads
    hps = LANES // HEAD_DIM
    qw = hps * group * HEAD_DIM
    k_blk0 = q_dim // LANES
    v_blk0 = (q_dim + kv_dim) // LANES
    kv_spec = lambda off, prev: pl.BlockSpec(
        (WINDOW, LANES), (lambda n, p: (jnp.maximum(n - 1, 0), off + p)) if prev else (lambda n, p: (n, off + p)))
    return pl.pallas_call(
        functools.partial(_attn_kernel, group=group, blocks_per_seq=seq // WINDOW, heads_per_step=hps),
        grid=(t // WINDOW, kv_heads // hps),
        in_specs=[pl.BlockSpec(memory_space=pltpu.SMEM),
                  pl.BlockSpec((WINDOW, qw), lambda n, p: (n, p)),
                  kv_spec(k_blk0, True), kv_spec(k_blk0, False), kv_spec(v_blk0, True), kv_spec(v_blk0, False)],
        out_specs=pl.BlockSpec((WINDOW, qw), lambda n, p: (n, p)),
        out_shape=jax.ShapeDtypeStruct((t, q_dim), BF16),
        compiler_params=_params(("arbitrary", "arbitrary")),
        name="swa_attention",
    )(sinks, qkv, qkv, qkv, qkv, qkv)


def _ret_kernel(q_ref, k_ref, v_ref, g_ref, dec_ref, xi_ref, zeta_ref, dc_ref, gain_ref, o_ref, state_ref):
    c = RET_CHUNK
    dv = v_ref.shape[1]
    state_ref[...] = jnp.zeros_like(state_ref)
    rep = dv // LANES
    xi = jnp.concatenate([xi_ref[0]] * rep, axis=1)
    zeta = jnp.concatenate([zeta_ref[0]] * (k_ref.shape[1] // LANES), axis=1)
    dc = jnp.concatenate([dc_ref[0]] * rep, axis=1)[:1]

    def body(i, carry):
        r = pl.multiple_of(i * c, c)
        q = q_ref[pl.ds(r, c), :]
        k = k_ref[pl.ds(r, c), :]
        v = v_ref[pl.ds(r, c), :]
        s = lax.dot_general(q, k, (((1,), (1,)), ((), ())), preferred_element_type=F32) * dec_ref[0]
        state = state_ref[...]
        o = jnp.dot(s.astype(BF16), v, preferred_element_type=F32)
        o = o + jnp.dot(q, state.astype(BF16), preferred_element_type=F32) * xi
        kz = (k.astype(F32) * zeta).astype(BF16)
        state_ref[...] = dc * state + lax.dot_general(kz, v, (((0,), (0,)), ((), ())), preferred_element_type=F32)
        mu = jnp.mean(o, axis=-1, keepdims=True)
        oc = o - mu
        var = jnp.mean(oc * oc, axis=-1, keepdims=True)
        on = oc * lax.rsqrt(var + GN_EPS) * gain_ref[0]
        o_ref[pl.ds(r, c), :] = (g_ref[pl.ds(r, c), :].astype(F32) * on).astype(BF16)
        return carry

    lax.fori_loop(0, q_ref.shape[0] // c, body, 0)


def _retention(proj, gn_gain, layer, consts, seq, d_model, v_width):
    t = proj.shape[0]
    heads = consts[0].shape[0]
    dk, dv = d_model // heads, v_width // heads
    k0, v0, g0 = d_model // dk, 2 * d_model // dv, (2 * d_model + v_width) // dv
    cspec = pl.BlockSpec((1, RET_CHUNK, LANES), lambda b, h: (h, 0, 0))
    return pl.pallas_call(
        _ret_kernel,
        grid=(t // seq, heads),
        in_specs=[pl.BlockSpec((seq, dk), lambda b, h: (b, h)),
                  pl.BlockSpec((seq, dk), lambda b, h: (b, k0 + h)),
                  pl.BlockSpec((seq, dv), lambda b, h: (b, v0 + h)),
                  pl.BlockSpec((seq, dv), lambda b, h: (b, g0 + h)),
                  cspec, cspec, cspec, cspec,
                  pl.BlockSpec((1, 1, dv), lambda b, h: (layer, 0, h))],
        out_specs=pl.BlockSpec((seq, dv), lambda b, h: (b, h)),
        out_shape=jax.ShapeDtypeStruct((t, v_width), BF16),
        scratch_shapes=[pltpu.VMEM((dk, dv), F32)],
        compiler_params=_params(("arbitrary", "arbitrary")),
        name="retention",
    )(proj, proj, proj, proj, *consts, gn_gain.reshape(gn_gain.shape[0], 1, v_width))


def _router_kernel(u_ref, w_ref, b_ref, cw_ref, ei_ref, rk_ref, cnt_ref, base_ref, *, n_experts):
    @pl.when(pl.program_id(0) == 0)
    def _():
        base_ref[...] = jnp.zeros_like(base_ref)

    logits = jnp.dot(u_ref[...], w_ref[0].astype(BF16), preferred_element_type=F32) + b_ref[0]
    bm = logits.shape[0]
    lane = lax.broadcasted_iota(I32, logits.shape, 1)
    work = jnp.where(lane < n_experts, logits, -jnp.inf)
    vals, hots, idxs = [], [], []
    for _ in range(TOP_K):
        mx = jnp.max(work, axis=-1, keepdims=True)
        idx = jnp.min(jnp.where(work == mx, lane, LANES), axis=-1, keepdims=True)
        hot = lane == idx
        work = jnp.where(hot, -jnp.inf, work)
        vals.append(mx)
        hots.append(hot)
        idxs.append(idx)
    es = [jnp.exp(v - vals[0]) for v in vals]
    inv = 1.0 / functools.reduce(lambda a, b: a + b, es)

    sel = functools.reduce(jnp.logical_or, hots).astype(F32)
    ri = lax.broadcasted_iota(I32, (bm, bm), 0)
    ci = lax.broadcasted_iota(I32, (bm, bm), 1)
    lower = (ci < ri).astype(BF16)
    base = base_ref[0:1, :]
    before = jnp.dot(lower, sel.astype(BF16), preferred_element_type=F32) + base

    cw = jnp.zeros(logits.shape, F32)
    ei = jnp.zeros(logits.shape, I32)
    rk = jnp.zeros(logits.shape, I32)
    for k in range(TOP_K):
        rank = jnp.sum(jnp.where(hots[k], before, 0.0), axis=-1, keepdims=True).astype(I32)
        cw = jnp.where(lane == k, es[k] * inv, cw)
        ei = jnp.where(lane == k, idxs[k], ei)
        rk = jnp.where(lane == k, rank, rk)
    cw_ref[...] = cw
    ei_ref[...] = ei
    rk_ref[...] = rk
    total = base + jnp.sum(sel, axis=0, keepdims=True)
    base_ref[0:1, :] = total
    cnt_ref[...] = jnp.broadcast_to(total, cnt_ref.shape).astype(I32)


def _router(u, rw_pad, rb_pad, layer, n_experts):
    t, d = u.shape
    bm = min(BM, t)
    tok_spec = pl.BlockSpec((bm, LANES), lambda m: (m, 0))
    return pl.pallas_call(
        functools.partial(_router_kernel, n_experts=n_experts),
        grid=(t // bm,),
        in_specs=[pl.BlockSpec((bm, d), lambda m: (m, 0)),
                  pl.BlockSpec((1, d, LANES), lambda m: (layer, 0, 0)),
                  pl.BlockSpec((1, 1, LANES), lambda m: (layer, 0, 0))],
        out_specs=[tok_spec, tok_spec, tok_spec, pl.BlockSpec((8, LANES), lambda m: (0, 0))],
        out_shape=[jax.ShapeDtypeStruct((t, LANES), F32), jax.ShapeDtypeStruct((t, LANES), I32),
                   jax.ShapeDtypeStruct((t, LANES), I32), jax.ShapeDtypeStruct((8, LANES), I32)],
        scratch_shapes=[pltpu.VMEM((8, LANES), F32)],
        compiler_params=_params(("arbitrary",)),
        name="router",
    )(u, rw_pad, rb_pad)


def _dispatch_plan(ei, rk, cnt, n_experts, t, tm):
    n_tiles = (TOP_K * t) // tm + n_experts
    rows = n_tiles * tm
    counts = cnt[0, :n_experts]
    padded = ((counts + tm - 1) // tm) * tm
    ends = jnp.cumsum(padded)
    starts = ends - padded
    pair_expert = ei[:, :TOP_K].reshape(-1)
    pos = starts[pair_expert] + rk[:, :TOP_K].reshape(-1)
    pair_of_row = jnp.zeros((rows,), I32).at[pos].set(jnp.arange(TOP_K * t, dtype=I32) + 1, unique_indices=True)
    valid = pair_of_row > 0
    pair = jnp.maximum(pair_of_row - 1, 0)
    tok, choice = pair // TOP_K, pair % TOP_K
    row = jnp.arange(rows, dtype=I32)
    src = jnp.where(valid, tok, 0)
    dst = jnp.where(valid, choice * t + tok, TOP_K * t + row % tm)
    tile_start = jnp.arange(n_tiles, dtype=I32) * tm
    tile_expert = jnp.minimum(jnp.sum(tile_start[:, None] >= ends[None, :], axis=1), n_experts - 1).astype(I32)
    tile_active = (tile_start < ends[-1]).astype(I32)
    return tile_expert, tile_active, src.reshape(n_tiles, 1, tm), dst.reshape(n_tiles, 1, tm)


def _expert_kernel(te_ref, act_ref, src_ref, srcn_ref, dst_ref, u_hbm, wgu_ref, bgu_ref, wd_ref, bd_ref, ys_hbm,
                   xbuf, obuf, wgu_bf, wd_bf, sem_in, sem_out):
    r = pl.program_id(0)
    last = pl.num_programs(0) - 1
    slot = r % 2
    tm = obuf.shape[0]

    def start_gather(idx_ref, s):
        for i in range(tm):
            pltpu.make_async_copy(u_hbm.at[pl.ds(idx_ref[0, 0, i], 1)], xbuf.at[s, pl.ds(i, 1)], sem_in.at[s]).start()

    def wait_scatter():
        pltpu.make_async_copy(obuf, obuf, sem_out.at[0]).wait()

    active = act_ref[r] != 0
    next_active = (r < last) & (act_ref[jnp.minimum(r + 1, last)] != 0)

    @pl.when(r == 0)
    def _():
        obuf[...] = jnp.zeros_like(obuf)
        spare = pltpu.make_async_copy(obuf, ys_hbm.at[pl.ds(ys_hbm.shape[0] - tm, tm)], sem_out.at[0])
        spare.start()
        spare.wait()

    @pl.when((r == 0) & active)
    def _():
        start_gather(src_ref, 0)

    for s in range(2):
        @pl.when(next_active & (slot == 1 - s))
        def _():
            start_gather(srcn_ref, s)

    @pl.when(active)
    def _():
        @pl.when((r == 0) | (te_ref[r] != te_ref[jnp.maximum(r - 1, 0)]))
        def _():
            _cast_weight(wgu_ref.at[0, 0], wgu_bf)
            _cast_weight(wd_ref.at[0, 0], wd_bf)

        pltpu.make_async_copy(xbuf.at[slot], xbuf.at[slot], sem_in.at[slot]).wait()
        h = jnp.dot(xbuf[slot].astype(BF16), wgu_bf[...], preferred_element_type=F32) + bgu_ref[0, 0]
        ff = h.shape[1] // 2
        gate = jnp.minimum(h[:, :ff], SWIGLU_LIMIT)
        up = jnp.clip(h[:, ff:], -SWIGLU_LIMIT, SWIGLU_LIMIT)
        act = (up + 1.0) * gate * jax.nn.sigmoid(SWIGLU_ALPHA * gate)
        y = jnp.dot(act.astype(BF16), wd_bf[...], preferred_element_type=F32) + bd_ref[0, 0]

        @pl.when(r > 0)
        def _():
            wait_scatter()

        obuf[...] = y

        for i in range(tm):
            pltpu.make_async_copy(obuf.at[pl.ds(i, 1)], ys_hbm.at[pl.ds(dst_ref[0, 0, i], 1)], sem_out.at[0]).start()

        @pl.when(jnp.logical_not(next_active))
        def _():
            wait_scatter()


def _experts(u32, w_gu, b_gu, w_down, b_down, layer, plan, tm):
    tile_expert, tile_active, src, dst = plan
    t, d = u32.shape
    n_tiles = src.shape[0]
    n_exp, ff2 = w_gu.shape[1], w_gu.shape[3]
    ff = ff2 // 2
    smem_spec = lambda f: pl.BlockSpec((1, 1, tm), f, memory_space=pltpu.SMEM)
    grid_spec = pltpu.PrefetchScalarGridSpec(
        num_scalar_prefetch=2,
        grid=(n_tiles,),
        in_specs=[smem_spec(lambda r, te, ac: (r, 0, 0)),
                  smem_spec(lambda r, te, ac: (jnp.minimum(r + 1, n_tiles - 1), 0, 0)),
                  smem_spec(lambda r, te, ac: (r, 0, 0)),
                  pl.BlockSpec(memory_space=pl.ANY),
                  pl.BlockSpec((1, 1, d, ff2), lambda r, te, ac: (layer, te[r], 0, 0)),
                  pl.BlockSpec((1, 1, 1, ff2), lambda r, te, ac: (layer, te[r], 0, 0)),
                  pl.BlockSpec((1, 1, ff, d), lambda r, te, ac: (layer, te[r], 0, 0)),
                  pl.BlockSpec((1, 1, 1, d), lambda r, te, ac: (layer, te[r], 0, 0))],
        out_specs=pl.BlockSpec(memory_space=pl.ANY),
        scratch_shapes=[pltpu.VMEM((2, tm, d), F32), pltpu.VMEM((tm, d), F32),
                        pltpu.VMEM((d, ff2), BF16), pltpu.VMEM((ff, d), BF16),
                        pltpu.SemaphoreType.DMA((2,)), pltpu.SemaphoreType.DMA((1,))])
    return pl.pallas_call(
        _expert_kernel,
        grid_spec=grid_spec,
        out_shape=jax.ShapeDtypeStruct((TOP_K * t + tm, d), F32),
        compiler_params=_params(("arbitrary",)),
        name="moe_experts",
    )(tile_expert, tile_active, src, src, dst, u32, w_gu, b_gu.reshape(b_gu.shape[0], n_exp, 1, ff2), w_down,
      b_down.reshape(b_down.shape[0], n_exp, 1, d))


def _attn_rope_tables(pos):
    half = ROPE_DIM // 2
    inv = ROPE_THETA ** (-jnp.arange(0, ROPE_DIM, 2, dtype=F32) / ROPE_DIM)
    ang = pos[:, None] * inv
    cos, sin = jnp.cos(ang), jnp.sin(ang)
    t = pos.shape[0]
    ones = jnp.ones((t, HEAD_DIM - ROPE_DIM), F32)
    zeros_h = jnp.zeros((t, half), F32)
    zeros_r = jnp.zeros((t, HEAD_DIM - ROPE_DIM), F32)
    c = jnp.concatenate([cos, cos, ones], axis=1)
    s_hi = jnp.concatenate([-sin, zeros_h, zeros_r], axis=1)
    s_lo = jnp.concatenate([zeros_h, sin, zeros_r], axis=1)
    rep = LANES // HEAD_DIM
    return tuple(jnp.tile(a, (1, rep)) for a in (c, s_hi, s_lo))


def _ret_rot_tables(pos, key_dim):
    inv = RET_ROT_BASE ** (-jnp.linspace(0.0, 1.0, key_dim // 2, dtype=F32))
    ang = pos[:, None] * inv
    cos, sin = jnp.cos(ang), jnp.sin(ang)
    c = jnp.repeat(cos, 2, axis=1)
    s = jnp.stack([-sin, sin], axis=-1).reshape(pos.shape[0], key_dim)
    return c, s


def _ret_decay_tables(heads):
    c = RET_CHUNK
    log_decay = jnp.log1p(-jnp.exp2(-5.0 - jnp.arange(heads, dtype=F32)))
    idx = jnp.arange(c, dtype=F32)
    rel = idx[:, None] - idx[None, :]
    decay_intra = jnp.where(rel >= 0, jnp.exp(jnp.maximum(rel, 0.0)[None] * log_decay[:, None, None]), 0.0)
    xi = jnp.exp((idx + 1.0)[None, :] * log_decay[:, None])
    zeta = jnp.exp((c - 1.0 - idx)[None, :] * log_decay[:, None])
    decay_chunk = jnp.exp(c * log_decay)
    lanes = lambda a: jnp.broadcast_to(a[:, :, None], (heads, c, LANES))
    return (decay_intra, lanes(xi), lanes(zeta), jnp.broadcast_to(decay_chunk[:, None, None], (heads, c, LANES)))


def kernel(x, c, positions, mod_w, mod_b, mod_layer, ln_gain, ln_bias, attn_w_qkv, attn_b_qkv, attn_sinks, attn_w_o,
           attn_b_o, ret_w_qkvg, ret_gn_gain, ret_w_o, router_w, router_b, expert_w_gu, expert_b_gu, expert_w_down,
           expert_b_down):
    bsz, seq, d = x.shape
    t = bsz * seq
    depth = mod_layer.shape[0]
    alpha = (2 * depth) ** 0.25
    n_exp = router_w.shape[2]
    q_dim = attn_w_o.shape[1]
    kv_dim = (attn_w_qkv.shape[2] - q_dim) // 2
    v_width = ret_w_o.shape[1]
    key_dim = d // RET_HEADS
    moe_tm = min(MOE_TM, seq)
    assert n_exp <= LANES and seq % WINDOW == 0 and seq % RET_CHUNK == 0 and (TOP_K * t) % moe_tm == 0

    mod = _mod_proj(c, mod_w, mod_b).reshape(bsz, N_MOD, d)
    m_all = mod[None] + mod_layer[:, None]
    mrow = lambda i, j: m_all[i, :, j][:, None, :]

    pos = positions.astype(F32).reshape(t)
    attn_tabs = _attn_rope_tables(pos)
    ret_tabs = _ret_rot_tables(pos, key_dim)
    ret_consts = _ret_decay_tables(RET_HEADS)

    rw_pad = jnp.zeros((depth, d, LANES), F32).at[:, :, :n_exp].set(router_w)
    rb_pad = jnp.zeros((depth, 1, LANES), F32).at[:, 0, :n_exp].set(router_b)

    xcur = x.reshape(t, d)
    (u,) = _ln_mod(xcur, seq, mod=(mrow(0, 1), mrow(0, 0)))
    for i in range(depth):
        j = i // 2
        if i % 2 == 0:
            qkv = _attn_qkv(u, attn_w_qkv, attn_b_qkv, j, attn_tabs, q_dim + kv_dim)
            o = _attention(qkv, attn_sinks[j], seq, q_dim, kv_dim)
            z = _residual_mm(o, attn_w_o, j, xcur, mrow(i, 2), seq, alpha, bias=attn_b_o)
        else:
            proj = _ret_proj(u, ret_w_qkvg, j, ret_tabs, d, v_width, key_dim)
            o = _retention(proj, ret_gn_gain, j, ret_consts, seq, d, v_width)
            z = _residual_mm(o, ret_w_o, j, xcur, mrow(i, 2), seq, alpha)
        xcur, u, u32 = _ln_mod(z, seq, ln=(ln_gain, ln_bias, i, 0), mod=(mrow(i, 4), mrow(i, 3)), emit_u32=True)

        cw, ei, rk, cnt = _router(u, rw_pad, rb_pad, i, n_exp)
        plan = _dispatch_plan(ei, rk, cnt, n_exp, t, moe_tm)
        ys = _experts(u32, expert_w_gu, expert_b_gu, expert_w_down, expert_b_down, i, plan, moe_tm)
        combine = (ys, cw, xcur, mrow(i, 5), alpha)
        if i + 1 < depth:
            xcur, u = _ln_mod(None, seq, ln=(ln_gain, ln_bias, i, 1), mod=(mrow(i + 1, 1), mrow(i + 1, 0)),
                              combine=combine)
        else:
            (xcur,) = _ln_mod(None, seq, ln=(ln_gain, ln_bias, i, 1), combine=combine)
    return xcur.reshape(bsz, seq, d)
```

```python
import functools

import jax
import jax.numpy as jnp
from jax import lax
from jax.experimental import pallas as pl
from jax.experimental.pallas import tpu as pltpu

HEAD_DIM = 64
KV_HEADS = 8
ROPE_DIM = 16
ROPE_THETA = 500000.0
WINDOW = 128
RET_HEADS = 16
RET_VALUE_FACTOR = 2
RET_CHUNK = 128
RET_ROT_BASE = 10000.0
GN_EPS = 1e-6
TOP_K = 4
SWIGLU_LIMIT = 7.0
SWIGLU_ALPHA = 1.702
LN_EPS = 1e-5
N_MOD = 6

LANES = 128
VMEM_LIMIT_BYTES = 56 * 1024 * 1024
BM = 1024
BN = 512
CAST_ROWS = 512
MOE_TM = 256
PROJ_CHUNKS = 4
RET_HEADS_PER_STEP = 2
RET_SEQ_BLOCK = 2048

F32 = jnp.float32
BF16 = jnp.bfloat16
I32 = jnp.int32


def _params(semantics, **kw):
    return pltpu.CompilerParams(dimension_semantics=semantics, vmem_limit_bytes=VMEM_LIMIT_BYTES, **kw)


def _cast_weight(w_ref, wbf_ref):
    k = w_ref.shape[0]
    rows = min(CAST_ROWS, k)

    def body(i, carry):
        r = pl.multiple_of(i * rows, rows)
        wbf_ref[pl.ds(r, rows), :] = w_ref[pl.ds(r, rows), :].astype(BF16)
        return carry

    lax.fori_loop(0, k // rows, body, 0)


def _mod_kernel(c_ref, w_ref, b_ref, o_ref):
    c = c_ref[...]
    s = (c * jax.nn.sigmoid(c)).astype(BF16)
    acc = jnp.dot(s, w_ref[...].astype(BF16), preferred_element_type=F32)
    o_ref[...] = acc + b_ref[...]


def _mod_proj(c, mod_w, mod_b):
    bsz, d = c.shape
    n = mod_w.shape[1]
    rows = 8
    c_pad = jnp.zeros((rows, d), F32).at[:bsz].set(c)
    bn = min(1024, n)
    out = pl.pallas_call(
        _mod_kernel,
        grid=(n // bn,),
        in_specs=[pl.BlockSpec((rows, d), lambda j: (0, 0)),
                  pl.BlockSpec((d, bn), lambda j: (0, j)),
                  pl.BlockSpec((1, bn), lambda j: (0, j))],
        out_specs=pl.BlockSpec((rows, bn), lambda j: (0, j)),
        out_shape=jax.ShapeDtypeStruct((rows, n), F32),
        compiler_params=_params(("arbitrary",)),
        name="mod_proj",
    )(c_pad, mod_w, mod_b.reshape(1, n))
    return out[:bsz]


def _ln_mod_kernel(*refs, apply_ln, emit_u, emit_u32, combine, alpha):
    it = iter(refs)
    if combine:
        y_refs = [next(it) for _ in range(TOP_K)]
        cw_ref, xres_ref, gate_ref = next(it), next(it), next(it)
    else:
        z_ref = next(it)
    if apply_ln:
        g_ref, b_ref = next(it), next(it)
    if emit_u:
        scale_ref, shift_ref = next(it), next(it)
    if apply_ln:
        x_ref = next(it)
    if emit_u:
        u_ref = next(it)
    if emit_u32:
        u32_ref = next(it)

    if combine:
        cw = cw_ref[...]
        y = y_refs[0][...] * cw[:, 0:1]
        for k in range(1, TOP_K):
            y = y + y_refs[k][...] * cw[:, k:k + 1]
        x = alpha * xres_ref[...] + (1.0 + gate_ref[0]) * y
    else:
        x = z_ref[...]
    if apply_ln:
        mu = jnp.mean(x, axis=-1, keepdims=True)
        xc = x - mu
        var = jnp.mean(xc * xc, axis=-1, keepdims=True)
        x = xc * lax.rsqrt(var + LN_EPS) * g_ref[0] + b_ref[0]
        x_ref[...] = x
    if emit_u:
        u = x * (1.0 + scale_ref[0]) + shift_ref[0]
        u_ref[...] = u.astype(BF16)
        if emit_u32:
            u32_ref[...] = u


def _ln_mod(z, seq, ln=None, mod=None, tm=256, emit_u32=False, combine=None):
    apply_ln, emit_u = ln is not None, mod is not None
    alpha = 1.0
    if combine is not None:
        ys, cw, xres, gate, alpha = combine
        t, d = xres.shape
        tm = min(tm // 2, seq)
        planes = t // tm
        args = [ys] * TOP_K + [cw, xres, gate]
        in_specs = [pl.BlockSpec((tm, d), lambda m, k=k: (k * planes + m, 0)) for k in range(TOP_K)]
        in_specs += [pl.BlockSpec((tm, LANES), lambda m: (m, 0)),
                     pl.BlockSpec((tm, d), lambda m: (m, 0)),
                     pl.BlockSpec((1, 1, d), lambda m: ((m * tm) // seq, 0, 0))]
    else:
        t, d = z.shape
        tm = min(tm, seq)
        args, in_specs = [z], [pl.BlockSpec((tm, d), lambda m: (m, 0))]
    if apply_ln:
        gain, bias, layer, slot = ln
        row = layer * gain.shape[1] + slot
        for a in (gain, bias):
            args.append(a.reshape(-1, 1, d))
            in_specs.append(pl.BlockSpec((1, 1, d), lambda m, row=row: (row, 0, 0)))
    if emit_u:
        for a in mod:
            args.append(a)
            in_specs.append(pl.BlockSpec((1, 1, d), lambda m: ((m * tm) // seq, 0, 0)))
    out_shape, out_specs = [], []
    for flag, dtype in ((apply_ln, F32), (emit_u, BF16), (emit_u32, F32)):
        if flag:
            out_shape.append(jax.ShapeDtypeStruct((t, d), dtype))
            out_specs.append(pl.BlockSpec((tm, d), lambda m: (m, 0)))
    return pl.pallas_call(
        functools.partial(_ln_mod_kernel, apply_ln=apply_ln, emit_u=emit_u, emit_u32=emit_u32,
                          combine=combine is not None, alpha=alpha),
        grid=(t // tm,),
        in_specs=in_specs,
        out_specs=out_specs,
        out_shape=out_shape,
        compiler_params=_params(("arbitrary",)),
        name="ln_mod",
    )(*args)


def _proj_kernel(*refs, kind, has_bias, k_from, k_scale, chunks):
    it = iter(refs)
    u_ref, w_ref = next(it), next(it)
    b_ref = next(it) if has_bias else None
    tabs = [next(it) for _ in range({"rope": 3, "rot": 2}.get(kind, 0))]
    o_ref, wbf_ref = next(it), next(it)
    j = pl.program_id(0)

    @pl.when(pl.program_id(1) == 0)
    def _():
        _cast_weight(w_ref.at[0], wbf_ref)

    bm, bn = o_ref.shape
    rows = bm // chunks
    groups = bn // LANES
    if kind == "rot":
        even = (lax.broadcasted_iota(I32, (1, LANES), 1) % 2) == 0
        scale = jnp.where(j >= k_from, k_scale, 1.0).astype(F32)
        tab_groups = tabs[0].shape[1] // LANES
    for mc in range(chunks):
        rs = slice(mc * rows, (mc + 1) * rows)
        acc = jnp.dot(u_ref[rs, :], wbf_ref[...], preferred_element_type=F32)
        if has_bias:
            acc = acc + b_ref[0]
        if kind == "plain":
            o_ref[rs, :] = acc.astype(BF16)
        elif kind == "silu":
            o_ref[rs, :] = (acc * jax.nn.sigmoid(acc)).astype(BF16)
        elif kind == "rope":
            c, s_hi, s_lo = (tab[rs, :] for tab in tabs)
            half = ROPE_DIM // 2
            for g in range(groups):
                sl = slice(g * LANES, (g + 1) * LANES)
                blk = acc[:, sl]
                out = blk * c + pltpu.roll(blk, LANES - half, 1) * s_hi + pltpu.roll(blk, half, 1) * s_lo
                o_ref[rs, sl] = out.astype(BF16)
        else:
            for g in range(groups):
                sl = slice(g * LANES, (g + 1) * LANES)
                tl = slice((g % tab_groups) * LANES, (g % tab_groups + 1) * LANES)
                blk = acc[:, sl]
                partner = jnp.where(even, pltpu.roll(blk, LANES - 1, 1), pltpu.roll(blk, 1, 1))
                out = (blk * tabs[0][rs, tl] + partner * tabs[1][rs, tl]) * scale
                o_ref[rs, sl] = out.astype(BF16)


def _proj(u, w, layer, col0, ncols, kind, bias=None, tabs=(), k_from_col=0, k_scale=1.0):
    t, d = u.shape
    bm, bn = min(BM, t), min(BN, ncols)
    assert col0 % bn == 0 and ncols % bn == 0 and t % bm == 0 and k_from_col % bn == 0
    j0 = col0 // bn
    chunks = PROJ_CHUNKS if bm % (PROJ_CHUNKS * 16) == 0 else 1
    args = [u, w]
    in_specs = [pl.BlockSpec((bm, d), lambda j, m: (m, 0)),
                pl.BlockSpec((1, d, bn), lambda j, m: (layer, 0, j0 + j))]
    if bias is not None:
        args.append(bias.reshape(bias.shape[0], 1, bias.shape[1]))
        in_specs.append(pl.BlockSpec((1, 1, bn), lambda j, m: (layer, 0, j0 + j)))
    for tab in tabs:
        assert bn % tab.shape[1] == 0
        args.append(tab)
        in_specs.append(pl.BlockSpec((bm, tab.shape[1]), lambda j, m: (m, 0)))
    return pl.pallas_call(
        functools.partial(_proj_kernel, kind=kind, has_bias=bias is not None, k_from=k_from_col // bn, k_scale=k_scale,
                          chunks=chunks),
        grid=(ncols // bn, t // bm),
        in_specs=in_specs,
        out_specs=pl.BlockSpec((bm, bn), lambda j, m: (m, j)),
        out_shape=jax.ShapeDtypeStruct((t, ncols), BF16),
        scratch_shapes=[pltpu.VMEM((d, bn), BF16)],
        compiler_params=_params(("arbitrary", "arbitrary")),
        name="proj_" + kind,
    )(*args)


def _residual_kernel(a_ref, w_ref, *refs, alpha, has_bias, chunks):
    it = iter(refs)
    b_ref = next(it) if has_bias else None
    x_ref, gate_ref, o_ref, wbf_ref = next(it), next(it), next(it), next(it)

    @pl.when(pl.program_id(1) == 0)
    def _():
        _cast_weight(w_ref.at[0], wbf_ref)

    rows = o_ref.shape[0] // chunks
    for mc in range(chunks):
        rs = slice(mc * rows, (mc + 1) * rows)
        y = jnp.dot(a_ref[rs, :], wbf_ref[...], preferred_element_type=F32)
        if has_bias:
            y = y + b_ref[0]
        o_ref[rs, :] = alpha * x_ref[rs, :] + (1.0 + gate_ref[0]) * y


def _residual_mm(a, w, layer, x, gate, seq, alpha, bias=None):
    t, k = a.shape
    d = w.shape[2]
    bn = min(BN, d)
    bm = min(BM if k * BM * 2 <= 8 * 1024 * 1024 else BM // 2, seq)
    chunks = PROJ_CHUNKS // 2 if bm % (PROJ_CHUNKS * 8) == 0 else 1
    args = [a, w]
    in_specs = [pl.BlockSpec((bm, k), lambda j, m: (m, 0)),
                pl.BlockSpec((1, k, bn), lambda j, m: (layer, 0, j), pipeline_mode=pl.Buffered(1))]
    if bias is not None:
        args.append(bias.reshape(bias.shape[0], 1, d))
        in_specs.append(pl.BlockSpec((1, 1, bn), lambda j, m: (layer, 0, j)))
    args += [x, gate]
    in_specs += [pl.BlockSpec((bm, bn), lambda j, m: (m, j)),
                 pl.BlockSpec((1, 1, bn), lambda j, m: ((m * bm) // seq, 0, j))]
    return pl.pallas_call(
        functools.partial(_residual_kernel, alpha=alpha, has_bias=bias is not None, chunks=chunks),
        grid=(d // bn, t // bm),
        in_specs=in_specs,
        out_specs=pl.BlockSpec((bm, bn), lambda j, m: (m, j)),
        out_shape=jax.ShapeDtypeStruct((t, d), F32),
        scratch_shapes=[pltpu.VMEM((k, bn), BF16)],
        compiler_params=_params(("arbitrary", "arbitrary")),
        name="residual_mm",
    )(*args)


def _attn_kernel(sink_ref, q_ref, kp_ref, kc_ref, vp_ref, vc_ref, o_ref, *, group, blocks_per_seq, heads_per_step):
    n = pl.program_id(0)
    p = pl.program_id(1)
    has_prev = (n % blocks_per_seq) != 0
    qi = lax.broadcasted_iota(jnp.int32, (WINDOW, 2 * WINDOW), 0)
    kj = lax.broadcasted_iota(jnp.int32, (WINDOW, 2 * WINDOW), 1)
    diff = qi + WINDOW - kj
    visible = (diff >= 0) & (diff < WINDOW) & (has_prev | (kj >= WINDOW))
    scale = HEAD_DIM ** -0.5
    for kk in range(heads_per_step):
        ks = slice(kk * HEAD_DIM, (kk + 1) * HEAD_DIM)
        kcat = jnp.concatenate([kp_ref[:, ks], kc_ref[:, ks]], axis=0)
        vcat = jnp.concatenate([vp_ref[:, ks], vc_ref[:, ks]], axis=0)
        for g in range(group):
            h = kk * group + g
            hs = slice(h * HEAD_DIM, (h + 1) * HEAD_DIM)
            s = lax.dot_general(q_ref[:, hs], kcat, (((1,), (1,)), ((), ())), preferred_element_type=F32) * scale
            s = jnp.where(visible, s, -jnp.inf)
            sink = sink_ref[p * heads_per_step * group + h]
            mx = jnp.maximum(jnp.max(s, axis=-1, keepdims=True), sink)
            e = jnp.exp(s - mx)
            denom = jnp.sum(e, axis=-1, keepdims=True) + jnp.exp(sink - mx)
            prob = (e * (1.0 / denom)).astype(BF16)
            o_ref[:, hs] = jnp.dot(prob, vcat, preferred_element_type=F32).astype(BF16)


def _attention(qk, v, sinks, seq, q_dim):
    t = qk.shape[0]
    kv_heads = v.shape[1] // HEAD_DIM
    group = (q_dim // HEAD_DIM) // kv_heads
    hps = LANES // HEAD_DIM
    qw = hps * group * HEAD_DIM
    k_blk0 = q_dim // LANES
    kv_spec = lambda off, prev: pl.BlockSpec(
        (WINDOW, LANES), (lambda n, p: (jnp.maximum(n - 1, 0), off + p)) if prev else (lambda n, p: (n, off + p)))
    return pl.pallas_call(
        functools.partial(_attn_kernel, group=group, blocks_per_seq=seq // WINDOW, heads_per_step=hps),
        grid=(t // WINDOW, kv_heads // hps),
        in_specs=[pl.BlockSpec(memory_space=pltpu.SMEM),
                  pl.BlockSpec((WINDOW, qw), lambda n, p: (n, p)),
                  kv_spec(k_blk0, True), kv_spec(k_blk0, False), kv_spec(0, True), kv_spec(0, False)],
        out_specs=pl.BlockSpec((WINDOW, qw), lambda n, p: (n, p)),
        out_shape=jax.ShapeDtypeStruct((t, q_dim), BF16),
        compiler_params=_params(("arbitrary", "arbitrary")),
        name="swa_attention",
    )(sinks, qk, qk, qk, v, v)


def _ret_kernel(q_ref, k_ref, v_ref, g_ref, dec_ref, xi_ref, zeta_ref, dc_ref, gain_ref, o_ref, state_ref, *, heads):
    c = RET_CHUNK
    dk = q_ref.shape[1] // heads
    dv = v_ref.shape[1] // heads

    @pl.when(pl.program_id(2) == 0)
    def _():
        state_ref[...] = jnp.zeros_like(state_ref)

    rep = dv // LANES
    xi = [jnp.concatenate([xi_ref[h]] * rep, axis=1) for h in range(heads)]
    zeta = [jnp.concatenate([zeta_ref[h]] * (dk // LANES), axis=1) for h in range(heads)]
    dc = [jnp.concatenate([dc_ref[h]] * rep, axis=1)[:1] for h in range(heads)]

    def body(i, carry):
        r = pl.multiple_of(i * c, c)
        for h in range(heads):
            ks = slice(h * dk, (h + 1) * dk)
            vs = slice(h * dv, (h + 1) * dv)
            q = q_ref[pl.ds(r, c), ks]
            k = k_ref[pl.ds(r, c), ks]
            v = v_ref[pl.ds(r, c), vs]
            s = lax.dot_general(q, k, (((1,), (1,)), ((), ())), preferred_element_type=F32) * dec_ref[h]
            state = state_ref[h]
            o = jnp.dot(s.astype(BF16), v, preferred_element_type=F32)
            o = o + jnp.dot(q, state.astype(BF16), preferred_element_type=F32) * xi[h]
            kz = (k.astype(F32) * zeta[h]).astype(BF16)
            state_ref[h] = dc[h] * state + lax.dot_general(kz, v, (((0,), (0,)), ((), ())),
                                                           preferred_element_type=F32)
            mu = jnp.mean(o, axis=-1, keepdims=True)
            oc = o - mu
            var = jnp.mean(oc * oc, axis=-1, keepdims=True)
            on = oc * lax.rsqrt(var + GN_EPS) * gain_ref[0][:, vs]
            o_ref[pl.ds(r, c), vs] = (g_ref[pl.ds(r, c), vs].astype(F32) * on).astype(BF16)
        return carry

    lax.fori_loop(0, q_ref.shape[0] // c, body, 0)


def _retention(qk, v, g, gn_gain, layer, consts, seq, d_model):
    t, v_width = v.shape
    heads = consts[0].shape[0]
    dk, dv = d_model // heads, v_width // heads
    hp = RET_HEADS_PER_STEP if heads % RET_HEADS_PER_STEP == 0 else 1
    sb = min(RET_SEQ_BLOCK, seq)
    nsb = seq // sb
    k0 = d_model // (hp * dk)
    cspec = pl.BlockSpec((hp, RET_CHUNK, LANES), lambda b, h, s: (h, 0, 0))
    return pl.pallas_call(
        functools.partial(_ret_kernel, heads=hp),
        grid=(t // seq, heads // hp, nsb),
        in_specs=[pl.BlockSpec((sb, hp * dk), lambda b, h, s: (b * nsb + s, h)),
                  pl.BlockSpec((sb, hp * dk), lambda b, h, s: (b * nsb + s, k0 + h)),
                  pl.BlockSpec((sb, hp * dv), lambda b, h, s: (b * nsb + s, h)),
                  pl.BlockSpec((sb, hp * dv), lambda b, h, s: (b * nsb + s, h)),
                  cspec, cspec, cspec, cspec,
                  pl.BlockSpec((1, 1, hp * dv), lambda b, h, s: (layer, 0, h))],
        out_specs=pl.BlockSpec((sb, hp * dv), lambda b, h, s: (b * nsb + s, h)),
        out_shape=jax.ShapeDtypeStruct((t, v_width), BF16),
        scratch_shapes=[pltpu.VMEM((hp, dk, dv), F32)],
        compiler_params=_params(("arbitrary", "arbitrary", "arbitrary")),
        name="retention",
    )(qk, qk, v, g, *consts, gn_gain.reshape(gn_gain.shape[0], 1, v_width))


def _router_kernel(u_ref, w_ref, b_ref, cw_ref, ei_ref, rk_ref, cnt_ref, base_ref, *, n_experts):
    @pl.when(pl.program_id(0) == 0)
    def _():
        base_ref[...] = jnp.zeros_like(base_ref)

    logits = jnp.dot(u_ref[...], w_ref[0].astype(BF16), preferred_element_type=F32) + b_ref[0]
    bm = logits.shape[0]
    lane = lax.broadcasted_iota(I32, logits.shape, 1)
    work = jnp.where(lane < n_experts, logits, -jnp.inf)
    vals, hots, idxs = [], [], []
    for _ in range(TOP_K):
        mx = jnp.max(work, axis=-1, keepdims=True)
        idx = jnp.min(jnp.where(work == mx, lane, LANES), axis=-1, keepdims=True)
        hot = lane == idx
        work = jnp.where(hot, -jnp.inf, work)
        vals.append(mx)
        hots.append(hot)
        idxs.append(idx)
    es = [jnp.exp(v - vals[0]) for v in vals]
    inv = 1.0 / functools.reduce(lambda a, b: a + b, es)

    sel = functools.reduce(jnp.logical_or, hots).astype(F32)
    ri = lax.broadcasted_iota(I32, (bm, bm), 0)
    ci = lax.broadcasted_iota(I32, (bm, bm), 1)
    lower = (ci < ri).astype(BF16)
    base = base_ref[0:1, :]
    before = jnp.dot(lower, sel.astype(BF16), preferred_element_type=F32) + base

    cw = jnp.zeros(logits.shape, F32)
    ei = jnp.zeros(logits.shape, I32)
    rk = jnp.zeros(logits.shape, I32)
    for k in range(TOP_K):
        rank = jnp.sum(jnp.where(hots[k], before, 0.0), axis=-1, keepdims=True).astype(I32)
        cw = jnp.where(lane == k, es[k] * inv, cw)
        ei = jnp.where(lane == k, idxs[k], ei)
        rk = jnp.where(lane == k, rank, rk)
    cw_ref[...] = cw
    ei_ref[...] = ei
    rk_ref[...] = rk
    total = base + jnp.sum(sel, axis=0, keepdims=True)
    base_ref[0:1, :] = total
    cnt_ref[...] = jnp.broadcast_to(total, cnt_ref.shape).astype(I32)


def _router(u, rw_pad, rb_pad, layer, n_experts):
    t, d = u.shape
    bm = min(BM, t)
    tok_spec = pl.BlockSpec((bm, LANES), lambda m: (m, 0))
    return pl.pallas_call(
        functools.partial(_router_kernel, n_experts=n_experts),
        grid=(t // bm,),
        in_specs=[pl.BlockSpec((bm, d), lambda m: (m, 0)),
                  pl.BlockSpec((1, d, LANES), lambda m: (layer, 0, 0)),
                  pl.BlockSpec((1, 1, LANES), lambda m: (layer, 0, 0))],
        out_specs=[tok_spec, tok_spec, tok_spec, pl.BlockSpec((8, LANES), lambda m: (0, 0))],
        out_shape=[jax.ShapeDtypeStruct((t, LANES), F32), jax.ShapeDtypeStruct((t, LANES), I32),
                   jax.ShapeDtypeStruct((t, LANES), I32), jax.ShapeDtypeStruct((8, LANES), I32)],
        scratch_shapes=[pltpu.VMEM((8, LANES), F32)],
        compiler_params=_params(("arbitrary",)),
        name="router",
    )(u, rw_pad, rb_pad)


def _invert_kernel(pos_ref, out_ref):
    def clear(i, carry):
        out_ref[i] = 0
        return carry
    lax.fori_loop(0, out_ref.shape[0], clear, 0, unroll=8)

    def place(p, carry):
        out_ref[pos_ref[p]] = p + 1
        return carry
    lax.fori_loop(0, pos_ref.shape[0], place, 0, unroll=8)


def _invert(pos, rows):
    return pl.pallas_call(
        _invert_kernel,
        in_specs=[pl.BlockSpec(memory_space=pltpu.SMEM)],
        out_specs=pl.BlockSpec(memory_space=pltpu.SMEM),
        out_shape=jax.ShapeDtypeStruct((rows,), I32),
        name="invert_rows",
    )(pos)


def _dispatch_plan(ei, rk, cnt, n_experts, t, tm):
    n_tiles = (TOP_K * t) // tm + n_experts
    rows = n_tiles * tm
    counts = cnt[0, :n_experts]
    padded = ((counts + tm - 1) // tm) * tm
    ends = jnp.cumsum(padded)
    starts = ends - padded
    pair_expert = ei[:, :TOP_K].reshape(-1)
    pos = starts[pair_expert] + rk[:, :TOP_K].reshape(-1)
    pair_of_row = _invert(pos, rows)
    valid = pair_of_row > 0
    pair = jnp.maximum(pair_of_row - 1, 0)
    tok, choice = pair // TOP_K, pair % TOP_K
    row = jnp.arange(rows, dtype=I32)
    src = jnp.where(valid, tok, 0)
    dst = jnp.where(valid, choice * t + tok, TOP_K * t + row % tm)
    tile_start = jnp.arange(n_tiles, dtype=I32) * tm
    tile_expert = jnp.minimum(jnp.sum(tile_start[:, None] >= ends[None, :], axis=1), n_experts - 1).astype(I32)
    tile_active = (tile_start < ends[-1]).astype(I32)
    return tile_expert, tile_active, src.reshape(n_tiles, 1, tm), dst.reshape(n_tiles, 1, tm)


def _expert_kernel(te_ref, act_ref, src_ref, srcn_ref, dst_ref, u_hbm, wgu_ref, bgu_ref, wd_ref, bd_ref, ys_hbm,
                   xbuf, obuf, wgu_bf, wd_bf, sem_in, sem_out):
    r = pl.program_id(0)
    last = pl.num_programs(0) - 1
    slot = r % 2
    tm = obuf.shape[0]

    def start_gather(idx_ref, s):
        for i in range(tm):
            pltpu.make_async_copy(u_hbm.at[pl.ds(idx_ref[0, 0, i], 1)], xbuf.at[s, pl.ds(i, 1)], sem_in.at[s]).start()

    def wait_scatter():
        pltpu.make_async_copy(obuf, obuf, sem_out.at[0]).wait()

    active = act_ref[r] != 0
    next_active = (r < last) & (act_ref[jnp.minimum(r + 1, last)] != 0)

    @pl.when(r == 0)
    def _():
        obuf[...] = jnp.zeros_like(obuf)
        spare = pltpu.make_async_copy(obuf, ys_hbm.at[pl.ds(ys_hbm.shape[0] - tm, tm)], sem_out.at[0])
        spare.start()
        spare.wait()

    @pl.when((r == 0) & active)
    def _():
        start_gather(src_ref, 0)

    for s in range(2):
        @pl.when(next_active & (slot == 1 - s))
        def _():
            start_gather(srcn_ref, s)

    @pl.when(active)
    def _():
        @pl.when((r == 0) | (te_ref[r] != te_ref[jnp.maximum(r - 1, 0)]))
        def _():
            _cast_weight(wgu_ref.at[0, 0], wgu_bf)
            _cast_weight(wd_ref.at[0, 0], wd_bf)

        pltpu.make_async_copy(xbuf.at[slot], xbuf.at[slot], sem_in.at[slot]).wait()
        h = jnp.dot(xbuf[slot].astype(BF16), wgu_bf[...], preferred_element_type=F32) + bgu_ref[0, 0]
        ff = h.shape[1] // 2
        gate = jnp.minimum(h[:, :ff], SWIGLU_LIMIT)
        up = jnp.clip(h[:, ff:], -SWIGLU_LIMIT, SWIGLU_LIMIT)
        act = (up + 1.0) * gate * jax.nn.sigmoid(SWIGLU_ALPHA * gate)
        y = jnp.dot(act.astype(BF16), wd_bf[...], preferred_element_type=F32) + bd_ref[0, 0]

        @pl.when(r > 0)
        def _():
            wait_scatter()

        obuf[...] = y

        for i in range(tm):
            pltpu.make_async_copy(obuf.at[pl.ds(i, 1)], ys_hbm.at[pl.ds(dst_ref[0, 0, i], 1)], sem_out.at[0]).start()

        @pl.when(jnp.logical_not(next_active))
        def _():
            wait_scatter()


def _experts(u32, w_gu, b_gu, w_down, b_down, layer, plan, tm):
    tile_expert, tile_active, src, dst = plan
    t, d = u32.shape
    n_tiles = src.shape[0]
    n_exp, ff2 = w_gu.shape[1], w_gu.shape[3]
    ff = ff2 // 2
    smem_spec = lambda f: pl.BlockSpec((1, 1, tm), f, memory_space=pltpu.SMEM)
    grid_spec = pltpu.PrefetchScalarGridSpec(
        num_scalar_prefetch=2,
        grid=(n_tiles,),
        in_specs=[smem_spec(lambda r, te, ac: (r, 0, 0)),
                  smem_spec(lambda r, te, ac: (jnp.minimum(r + 1, n_tiles - 1), 0, 0)),
                  smem_spec(lambda r, te, ac: (r, 0, 0)),
                  pl.BlockSpec(memory_space=pl.ANY),
                  pl.BlockSpec((1, 1, d, ff2), lambda r, te, ac: (layer, te[r], 0, 0)),
                  pl.BlockSpec((1, 1, 1, ff2), lambda r, te, ac: (layer, te[r], 0, 0)),
                  pl.BlockSpec((1, 1, ff, d), lambda r, te, ac: (layer, te[r], 0, 0)),
                  pl.BlockSpec((1, 1, 1, d), lambda r, te, ac: (layer, te[r], 0, 0))],
        out_specs=pl.BlockSpec(memory_space=pl.ANY),
        scratch_shapes=[pltpu.VMEM((2, tm, d), F32), pltpu.VMEM((tm, d), F32),
                        pltpu.VMEM((d, ff2), BF16), pltpu.VMEM((ff, d), BF16),
                        pltpu.SemaphoreType.DMA((2,)), pltpu.SemaphoreType.DMA((1,))])
    return pl.pallas_call(
        _expert_kernel,
        grid_spec=grid_spec,
        out_shape=jax.ShapeDtypeStruct((TOP_K * t + tm, d), F32),
        compiler_params=_params(("arbitrary",)),
        name="moe_experts",
    )(tile_expert, tile_active, src, src, dst, u32, w_gu, b_gu.reshape(b_gu.shape[0], n_exp, 1, ff2), w_down,
      b_down.reshape(b_down.shape[0], n_exp, 1, d))


def _attn_rope_tables(pos):
    half = ROPE_DIM // 2
    inv = ROPE_THETA ** (-jnp.arange(0, ROPE_DIM, 2, dtype=F32) / ROPE_DIM)
    ang = pos[:, None] * inv
    cos, sin = jnp.cos(ang), jnp.sin(ang)
    t = pos.shape[0]
    ones = jnp.ones((t, HEAD_DIM - ROPE_DIM), F32)
    zeros_h = jnp.zeros((t, half), F32)
    zeros_r = jnp.zeros((t, HEAD_DIM - ROPE_DIM), F32)
    c = jnp.concatenate([cos, cos, ones], axis=1)
    s_hi = jnp.concatenate([-sin, zeros_h, zeros_r], axis=1)
    s_lo = jnp.concatenate([zeros_h, sin, zeros_r], axis=1)
    rep = LANES // HEAD_DIM
    return tuple(jnp.tile(a, (1, rep)) for a in (c, s_hi, s_lo))


def _ret_rot_tables(pos, key_dim):
    inv = RET_ROT_BASE ** (-jnp.linspace(0.0, 1.0, key_dim // 2, dtype=F32))
    ang = pos[:, None] * inv
    cos, sin = jnp.cos(ang), jnp.sin(ang)
    c = jnp.repeat(cos, 2, axis=1)
    s = jnp.stack([-sin, sin], axis=-1).reshape(pos.shape[0], key_dim)
    return c, s


def _ret_decay_tables(heads):
    c = RET_CHUNK
    log_decay = jnp.log1p(-jnp.exp2(-5.0 - jnp.arange(heads, dtype=F32)))
    idx = jnp.arange(c, dtype=F32)
    rel = idx[:, None] - idx[None, :]
    decay_intra = jnp.where(rel >= 0, jnp.exp(jnp.maximum(rel, 0.0)[None] * log_decay[:, None, None]), 0.0)
    xi = jnp.exp((idx + 1.0)[None, :] * log_decay[:, None])
    zeta = jnp.exp((c - 1.0 - idx)[None, :] * log_decay[:, None])
    decay_chunk = jnp.exp(c * log_decay)
    lanes = lambda a: jnp.broadcast_to(a[:, :, None], (heads, c, LANES))
    return (decay_intra, lanes(xi), lanes(zeta), jnp.broadcast_to(decay_chunk[:, None, None], (heads, c, LANES)))


def kernel(x, c, positions, mod_w, mod_b, mod_layer, ln_gain, ln_bias, attn_w_qkv, attn_b_qkv, attn_sinks, attn_w_o,
           attn_b_o, ret_w_qkvg, ret_gn_gain, ret_w_o, router_w, router_b, expert_w_gu, expert_b_gu, expert_w_down,
           expert_b_down):
    bsz, seq, d = x.shape
    t = bsz * seq
    depth = mod_layer.shape[0]
    alpha = (2 * depth) ** 0.25
    n_exp = router_w.shape[2]
    q_dim = attn_w_o.shape[1]
    kv_dim = (attn_w_qkv.shape[2] - q_dim) // 2
    v_width = ret_w_o.shape[1]
    key_dim = d // RET_HEADS
    moe_tm = min(MOE_TM, seq)
    assert n_exp <= LANES and seq % WINDOW == 0 and seq % RET_CHUNK == 0 and (TOP_K * t) % moe_tm == 0

    mod = _mod_proj(c, mod_w, mod_b).reshape(bsz, N_MOD, d)
    m_all = mod[None] + mod_layer[:, None]
    mrow = lambda i, j: m_all[i, :, j][:, None, :]

    pos = positions.astype(F32).reshape(t)
    attn_tabs = _attn_rope_tables(pos)
    ret_tabs = _ret_rot_tables(pos, key_dim)
    ret_consts = _ret_decay_tables(RET_HEADS)

    rw_pad = jnp.zeros((depth, d, LANES), F32).at[:, :, :n_exp].set(router_w)
    rb_pad = jnp.zeros((depth, 1, LANES), F32).at[:, 0, :n_exp].set(router_b)

    xcur = x.reshape(t, d)
    (u,) = _ln_mod(xcur, seq, mod=(mrow(0, 1), mrow(0, 0)))
    for i in range(depth):
        j = i // 2
        if i % 2 == 0:
            qk = _proj(u, attn_w_qkv, j, 0, q_dim + kv_dim, "rope", bias=attn_b_qkv, tabs=attn_tabs)
            v = _proj(u, attn_w_qkv, j, q_dim + kv_dim, kv_dim, "plain", bias=attn_b_qkv)
            o = _attention(qk, v, attn_sinks[j], seq, q_dim)
            z = _residual_mm(o, attn_w_o, j, xcur, mrow(i, 2), seq, alpha, bias=attn_b_o)
        else:
            qk = _proj(u, ret_w_qkvg, j, 0, 2 * d, "rot", tabs=ret_tabs, k_from_col=d, k_scale=key_dim ** -0.5)
            v = _proj(u, ret_w_qkvg, j, 2 * d, v_width, "plain")
            g = _proj(u, ret_w_qkvg, j, 2 * d + v_width, v_width, "silu")
            o = _retention(qk, v, g, ret_gn_gain, j, ret_consts, seq, d)
            z = _residual_mm(o, ret_w_o, j, xcur, mrow(i, 2), seq, alpha)
        xcur, u, u32 = _ln_mod(z, seq, ln=(ln_gain, ln_bias, i, 0), mod=(mrow(i, 4), mrow(i, 3)), emit_u32=True)

        cw, ei, rk, cnt = _router(u, rw_pad, rb_pad, i, n_exp)
        plan = _dispatch_plan(ei, rk, cnt, n_exp, t, moe_tm)
        ys = _experts(u32, expert_w_gu, expert_b_gu, expert_w_down, expert_b_down, i, plan, moe_tm)
        combine = (ys, cw, xcur, mrow(i, 5), alpha)
        if i + 1 < depth:
            xcur, u = _ln_mod(None, seq, ln=(ln_gain, ln_bias, i, 1), mod=(mrow(i + 1, 1), mrow(i + 1, 0)),
                              combine=combine)
        else:
            (xcur,) = _ln_mod(None, seq, ln=(ln_gain, ln_bias, i, 1), combine=combine)
    return xcur.reshape(bsz, seq, d)
```

```python
import functools

import jax
import jax.numpy as jnp
from jax import lax
from jax.experimental import pallas as pl
from jax.experimental.pallas import tpu as pltpu

HEAD_DIM = 64
KV_HEADS = 8
ROPE_DIM = 16
ROPE_THETA = 500000.0
WINDOW = 128
RET_HEADS = 16
RET_VALUE_FACTOR = 2
RET_CHUNK = 128
RET_ROT_BASE = 10000.0
GN_EPS = 1e-6
TOP_K = 4
SWIGLU_LIMIT = 7.0
SWIGLU_ALPHA = 1.702
LN_EPS = 1e-5
N_MOD = 6

LANES = 128
VMEM_LIMIT_BYTES = 56 * 1024 * 1024
BM = 1024
BN = 512
CAST_ROWS = 512
MOE_TM = 256
PROJ_CHUNKS = 4
RET_HEADS_PER_STEP = 2
RET_SEQ_BLOCK = 2048

F32 = jnp.float32
BF16 = jnp.bfloat16
I32 = jnp.int32
U32 = jnp.uint32


def _params(semantics, **kw):
    return pltpu.CompilerParams(dimension_semantics=semantics, vmem_limit_bytes=VMEM_LIMIT_BYTES, **kw)


def _pack_halves(x):
    half = x.shape[1] // 2
    lo = lax.bitcast_convert_type(x[:, :half].astype(BF16).astype(F32), U32)
    hi = lax.bitcast_convert_type(x[:, half:].astype(BF16).astype(F32), U32)
    return (lo >> 16) | hi


def _unpack_halves(w):
    lo = lax.bitcast_convert_type(w << 16, F32)
    hi = lax.bitcast_convert_type(w & jnp.uint32(0xFFFF0000), F32)
    return lo, hi


def _cast_weight(w_ref, wbf_ref):
    k = w_ref.shape[0]
    rows = min(CAST_ROWS, k)

    def body(i, carry):
        r = pl.multiple_of(i * rows, rows)
        wbf_ref[pl.ds(r, rows), :] = w_ref[pl.ds(r, rows), :].astype(BF16)
        return carry

    lax.fori_loop(0, k // rows, body, 0)


def _mod_kernel(c_ref, w_ref, b_ref, o_ref):
    c = c_ref[...]
    s = (c * jax.nn.sigmoid(c)).astype(BF16)
    acc = jnp.dot(s, w_ref[...].astype(BF16), preferred_element_type=F32)
    o_ref[...] = acc + b_ref[...]


def _mod_proj(c, mod_w, mod_b):
    bsz, d = c.shape
    n = mod_w.shape[1]
    rows = 8
    c_pad = jnp.zeros((rows, d), F32).at[:bsz].set(c)
    bn = min(1024, n)
    out = pl.pallas_call(
        _mod_kernel,
        grid=(n // bn,),
        in_specs=[pl.BlockSpec((rows, d), lambda j: (0, 0)),
                  pl.BlockSpec((d, bn), lambda j: (0, j)),
                  pl.BlockSpec((1, bn), lambda j: (0, j))],
        out_specs=pl.BlockSpec((rows, bn), lambda j: (0, j)),
        out_shape=jax.ShapeDtypeStruct((rows, n), F32),
        compiler_params=_params(("arbitrary",)),
        name="mod_proj",
    )(c_pad, mod_w, mod_b.reshape(1, n))
    return out[:bsz]


def _ln_mod_kernel(*refs, apply_ln, emit_u, emit_u32, combine, alpha):
    it = iter(refs)
    if combine:
        y_refs = [next(it) for _ in range(TOP_K)]
        cw_ref, xres_ref, gate_ref = next(it), next(it), next(it)
    else:
        z_ref = next(it)
    if apply_ln:
        g_ref, b_ref = next(it), next(it)
    if emit_u:
        scale_ref, shift_ref = next(it), next(it)
    if apply_ln:
        x_ref = next(it)
    if emit_u:
        u_ref = next(it)
    if emit_u32:
        u32_ref = next(it)

    if combine:
        cw = cw_ref[...]
        y_lo, y_hi = None, None
        for k in range(TOP_K):
            lo, hi = _unpack_halves(y_refs[k][...])
            wk = cw[:, k:k + 1]
            y_lo = lo * wk if y_lo is None else y_lo + lo * wk
            y_hi = hi * wk if y_hi is None else y_hi + hi * wk
        y = jnp.concatenate([y_lo, y_hi], axis=1)
        x = alpha * xres_ref[...] + (1.0 + gate_ref[0]) * y
    else:
        x = z_ref[...]
    if apply_ln:
        mu = jnp.mean(x, axis=-1, keepdims=True)
        xc = x - mu
        var = jnp.mean(xc * xc, axis=-1, keepdims=True)
        x = xc * lax.rsqrt(var + LN_EPS) * g_ref[0] + b_ref[0]
        x_ref[...] = x
    if emit_u:
        u = x * (1.0 + scale_ref[0]) + shift_ref[0]
        u_ref[...] = u.astype(BF16)
        if emit_u32:
            u32_ref[...] = _pack_halves(u)


def _ln_mod(z, seq, ln=None, mod=None, tm=256, emit_u32=False, combine=None):
    apply_ln, emit_u = ln is not None, mod is not None
    alpha = 1.0
    if combine is not None:
        ys, cw, xres, gate, alpha = combine
        t, d = xres.shape
        tm = min(tm // 2, seq)
        planes = t // tm
        args = [ys] * TOP_K + [cw, xres, gate]
        in_specs = [pl.BlockSpec((tm, d // 2), lambda m, k=k: (k * planes + m, 0)) for k in range(TOP_K)]
        in_specs += [pl.BlockSpec((tm, LANES), lambda m: (m, 0)),
                     pl.BlockSpec((tm, d), lambda m: (m, 0)),
                     pl.BlockSpec((1, 1, d), lambda m: ((m * tm) // seq, 0, 0))]
    else:
        t, d = z.shape
        tm = min(tm, seq)
        args, in_specs = [z], [pl.BlockSpec((tm, d), lambda m: (m, 0))]
    if apply_ln:
        gain, bias, layer, slot = ln
        row = layer * gain.shape[1] + slot
        for a in (gain, bias):
            args.append(a.reshape(-1, 1, d))
            in_specs.append(pl.BlockSpec((1, 1, d), lambda m, row=row: (row, 0, 0)))
    if emit_u:
        for a in mod:
            args.append(a)
            in_specs.append(pl.BlockSpec((1, 1, d), lambda m: ((m * tm) // seq, 0, 0)))
    out_shape, out_specs = [], []
    for flag, dtype, width in ((apply_ln, F32, d), (emit_u, BF16, d), (emit_u32, U32, d // 2)):
        if flag:
            out_shape.append(jax.ShapeDtypeStruct((t, width), dtype))
            out_specs.append(pl.BlockSpec((tm, width), lambda m: (m, 0)))
    return pl.pallas_call(
        functools.partial(_ln_mod_kernel, apply_ln=apply_ln, emit_u=emit_u, emit_u32=emit_u32,
                          combine=combine is not None, alpha=alpha),
        grid=(t // tm,),
        in_specs=in_specs,
        out_specs=out_specs,
        out_shape=out_shape,
        compiler_params=_params(("arbitrary",)),
        name="ln_mod",
    )(*args)


def _proj_kernel(*refs, kind, has_bias, k_from, k_scale, chunks):
    it = iter(refs)
    u_ref, w_ref = next(it), next(it)
    b_ref = next(it) if has_bias else None
    tabs = [next(it) for _ in range({"rope": 3, "rot": 2}.get(kind, 0))]
    o_ref, wbf_ref = next(it), next(it)
    j = pl.program_id(0)

    @pl.when(pl.program_id(1) == 0)
    def _():
        _cast_weight(w_ref.at[0], wbf_ref)

    bm, bn = o_ref.shape
    rows = bm // chunks
    groups = bn // LANES
    if kind == "rot":
        even = (lax.broadcasted_iota(I32, (1, LANES), 1) % 2) == 0
        scale = jnp.where(j >= k_from, k_scale, 1.0).astype(F32)
        tab_groups = tabs[0].shape[1] // LANES
    for mc in range(chunks):
        rs = slice(mc * rows, (mc + 1) * rows)
        acc = jnp.dot(u_ref[rs, :], wbf_ref[...], preferred_element_type=F32)
        if has_bias:
            acc = acc + b_ref[0]
        if kind == "plain":
            o_ref[rs, :] = acc.astype(BF16)
        elif kind == "silu":
            o_ref[rs, :] = (acc * jax.nn.sigmoid(acc)).astype(BF16)
        elif kind == "rope":
            c, s_hi, s_lo = (tab[rs, :] for tab in tabs)
            half = ROPE_DIM // 2
            for g in range(groups):
                sl = slice(g * LANES, (g + 1) * LANES)
                blk = acc[:, sl]
                out = blk * c + pltpu.roll(blk, LANES - half, 1) * s_hi + pltpu.roll(blk, half, 1) * s_lo
                o_ref[rs, sl] = out.astype(BF16)
        else:
            for g in range(groups):
                sl = slice(g * LANES, (g + 1) * LANES)
                tl = slice((g % tab_groups) * LANES, (g % tab_groups + 1) * LANES)
                blk = acc[:, sl]
                partner = jnp.where(even, pltpu.roll(blk, LANES - 1, 1), pltpu.roll(blk, 1, 1))
                out = (blk * tabs[0][rs, tl] + partner * tabs[1][rs, tl]) * scale
                o_ref[rs, sl] = out.astype(BF16)


def _proj(u, w, layer, col0, ncols, kind, bias=None, tabs=(), k_from_col=0, k_scale=1.0):
    t, d = u.shape
    bm, bn = min(BM, t), min(BN, ncols)
    assert col0 % bn == 0 and ncols % bn == 0 and t % bm == 0 and k_from_col % bn == 0
    j0 = col0 // bn
    chunks = PROJ_CHUNKS if bm % (PROJ_CHUNKS * 16) == 0 else 1
    args = [u, w]
    in_specs = [pl.BlockSpec((bm, d), lambda j, m: (m, 0)),
                pl.BlockSpec((1, d, bn), lambda j, m: (layer, 0, j0 + j))]
    if bias is not None:
        args.append(bias.reshape(bias.shape[0], 1, bias.shape[1]))
        in_specs.append(pl.BlockSpec((1, 1, bn), lambda j, m: (layer, 0, j0 + j)))
    for tab in tabs:
        assert bn % tab.shape[1] == 0
        args.append(tab)
        in_specs.append(pl.BlockSpec((bm, tab.shape[1]), lambda j, m: (m, 0)))
    return pl.pallas_call(
        functools.partial(_proj_kernel, kind=kind, has_bias=bias is not None, k_from=k_from_col // bn, k_scale=k_scale,
                          chunks=chunks),
        grid=(ncols // bn, t // bm),
        in_specs=in_specs,
        out_specs=pl.BlockSpec((bm, bn), lambda j, m: (m, j)),
        out_shape=jax.ShapeDtypeStruct((t, ncols), BF16),
        scratch_shapes=[pltpu.VMEM((d, bn), BF16)],
        compiler_params=_params(("arbitrary", "arbitrary")),
        name="proj_" + kind,
    )(*args)


def _residual_kernel(a_ref, w_ref, *refs, alpha, has_bias, chunks):
    it = iter(refs)
    b_ref = next(it) if has_bias else None
    x_ref, gate_ref, o_ref, wbf_ref = next(it), next(it), next(it), next(it)

    @pl.when(pl.program_id(1) == 0)
    def _():
        _cast_weight(w_ref.at[0], wbf_ref)

    rows = o_ref.shape[0] // chunks
    for mc in range(chunks):
        rs = slice(mc * rows, (mc + 1) * rows)
        y = jnp.dot(a_ref[rs, :], wbf_ref[...], preferred_element_type=F32)
        if has_bias:
            y = y + b_ref[0]
        o_ref[rs, :] = alpha * x_ref[rs, :] + (1.0 + gate_ref[0]) * y


def _residual_mm(a, w, layer, x, gate, seq, alpha, bias=None):
    t, k = a.shape
    d = w.shape[2]
    bn = min(BN, d)
    bm = min(BM if k * BM * 2 <= 8 * 1024 * 1024 else BM // 2, seq)
    chunks = PROJ_CHUNKS // 2 if bm % (PROJ_CHUNKS * 8) == 0 else 1
    args = [a, w]
    in_specs = [pl.BlockSpec((bm, k), lambda j, m: (m, 0)),
                pl.BlockSpec((1, k, bn), lambda j, m: (layer, 0, j), pipeline_mode=pl.Buffered(1))]
    if bias is not None:
        args.append(bias.reshape(bias.shape[0], 1, d))
        in_specs.append(pl.BlockSpec((1, 1, bn), lambda j, m: (layer, 0, j)))
    args += [x, gate]
    in_specs += [pl.BlockSpec((bm, bn), lambda j, m: (m, j)),
                 pl.BlockSpec((1, 1, bn), lambda j, m: ((m * bm) // seq, 0, j))]
    return pl.pallas_call(
        functools.partial(_residual_kernel, alpha=alpha, has_bias=bias is not None, chunks=chunks),
        grid=(d // bn, t // bm),
        in_specs=in_specs,
        out_specs=pl.BlockSpec((bm, bn), lambda j, m: (m, j)),
        out_shape=jax.ShapeDtypeStruct((t, d), F32),
        scratch_shapes=[pltpu.VMEM((k, bn), BF16)],
        compiler_params=_params(("arbitrary", "arbitrary")),
        name="residual_mm",
    )(*args)


def _attn_kernel(sink_ref, q_ref, kp_ref, kc_ref, vp_ref, vc_ref, o_ref, *, group, blocks_per_seq, heads_per_step):
    n = pl.program_id(0)
    p = pl.program_id(1)
    has_prev = (n % blocks_per_seq) != 0
    qi = lax.broadcasted_iota(jnp.int32, (WINDOW, 2 * WINDOW), 0)
    kj = lax.broadcasted_iota(jnp.int32, (WINDOW, 2 * WINDOW), 1)
    diff = qi + WINDOW - kj
    visible = (diff >= 0) & (diff < WINDOW) & (has_prev | (kj >= WINDOW))
    scale = HEAD_DIM ** -0.5
    for kk in range(heads_per_step):
        ks = slice(kk * HEAD_DIM, (kk + 1) * HEAD_DIM)
        kcat = jnp.concatenate([kp_ref[:, ks], kc_ref[:, ks]], axis=0)
        vcat = jnp.concatenate([vp_ref[:, ks], vc_ref[:, ks]], axis=0)
        for g in range(group):
            h = kk * group + g
            hs = slice(h * HEAD_DIM, (h + 1) * HEAD_DIM)
            s = lax.dot_general(q_ref[:, hs], kcat, (((1,), (1,)), ((), ())), preferred_element_type=F32) * scale
            s = jnp.where(visible, s, -jnp.inf)
            sink = sink_ref[p * heads_per_step * group + h]
            mx = jnp.maximum(jnp.max(s, axis=-1, keepdims=True), sink)
            e = jnp.exp(s - mx)
            denom = jnp.sum(e, axis=-1, keepdims=True) + jnp.exp(sink - mx)
            prob = (e * (1.0 / denom)).astype(BF16)
            o_ref[:, hs] = jnp.dot(prob, vcat, preferred_element_type=F32).astype(BF16)


def _attention(qk, v, sinks, seq, q_dim):
    t = qk.shape[0]
    kv_heads = v.shape[1] // HEAD_DIM
    group = (q_dim // HEAD_DIM) // kv_heads
    hps = LANES // HEAD_DIM
    qw = hps * group * HEAD_DIM
    k_blk0 = q_dim // LANES
    kv_spec = lambda off, prev: pl.BlockSpec(
        (WINDOW, LANES), (lambda n, p: (jnp.maximum(n - 1, 0), off + p)) if prev else (lambda n, p: (n, off + p)))
    return pl.pallas_call(
        functools.partial(_attn_kernel, group=group, blocks_per_seq=seq // WINDOW, heads_per_step=hps),
        grid=(t // WINDOW, kv_heads // hps),
        in_specs=[pl.BlockSpec(memory_space=pltpu.SMEM),
                  pl.BlockSpec((WINDOW, qw), lambda n, p: (n, p)),
                  kv_spec(k_blk0, True), kv_spec(k_blk0, False), kv_spec(0, True), kv_spec(0, False)],
        out_specs=pl.BlockSpec((WINDOW, qw), lambda n, p: (n, p)),
        out_shape=jax.ShapeDtypeStruct((t, q_dim), BF16),
        compiler_params=_params(("arbitrary", "arbitrary")),
        name="swa_attention",
    )(sinks, qk, qk, qk, v, v)


def _ret_kernel(q_ref, k_ref, v_ref, g_ref, dec_ref, xi_ref, zeta_ref, dc_ref, gain_ref, o_ref, state_ref, *, heads):
    c = RET_CHUNK
    dk = q_ref.shape[1] // heads
    dv = v_ref.shape[1] // heads

    @pl.when(pl.program_id(2) == 0)
    def _():
        state_ref[...] = jnp.zeros_like(state_ref)

    rep = dv // LANES
    xi = [jnp.concatenate([xi_ref[h]] * rep, axis=1) for h in range(heads)]
    zeta = [jnp.concatenate([zeta_ref[h]] * (dk // LANES), axis=1) for h in range(heads)]
    dc = [jnp.concatenate([dc_ref[h]] * rep, axis=1)[:1] for h in range(heads)]

    def body(i, carry):
        r = pl.multiple_of(i * c, c)
        for h in range(heads):
            ks = slice(h * dk, (h + 1) * dk)
            vs = slice(h * dv, (h + 1) * dv)
            q = q_ref[pl.ds(r, c), ks]
            k = k_ref[pl.ds(r, c), ks]
            v = v_ref[pl.ds(r, c), vs]
            s = lax.dot_general(q, k, (((1,), (1,)), ((), ())), preferred_element_type=F32) * dec_ref[h]
            state = state_ref[h]
            o = jnp.dot(s.astype(BF16), v, preferred_element_type=F32)
            o = o + jnp.dot(q, state.astype(BF16), preferred_element_type=F32) * xi[h]
            kz = (k.astype(F32) * zeta[h]).astype(BF16)
            state_ref[h] = dc[h] * state + lax.dot_general(kz, v, (((0,), (0,)), ((), ())),
                                                           preferred_element_type=F32)
            mu = jnp.mean(o, axis=-1, keepdims=True)
            oc = o - mu
            var = jnp.mean(oc * oc, axis=-1, keepdims=True)
            on = oc * lax.rsqrt(var + GN_EPS) * gain_ref[0][:, vs]
            o_ref[pl.ds(r, c), vs] = (g_ref[pl.ds(r, c), vs].astype(F32) * on).astype(BF16)
        return carry

    lax.fori_loop(0, q_ref.shape[0] // c, body, 0)


def _retention(qk, v, g, gn_gain, layer, consts, seq, d_model):
    t, v_width = v.shape
    heads = consts[0].shape[0]
    dk, dv = d_model // heads, v_width // heads
    hp = RET_HEADS_PER_STEP if heads % RET_HEADS_PER_STEP == 0 else 1
    sb = min(RET_SEQ_BLOCK, seq)
    nsb = seq // sb
    k0 = d_model // (hp * dk)
    cspec = pl.BlockSpec((hp, RET_CHUNK, LANES), lambda b, h, s: (h, 0, 0))
    return pl.pallas_call(
        functools.partial(_ret_kernel, heads=hp),
        grid=(t // seq, heads // hp, nsb),
        in_specs=[pl.BlockSpec((sb, hp * dk), lambda b, h, s: (b * nsb + s, h)),
                  pl.BlockSpec((sb, hp * dk), lambda b, h, s: (b * nsb + s, k0 + h)),
                  pl.BlockSpec((sb, hp * dv), lambda b, h, s: (b * nsb + s, h)),
                  pl.BlockSpec((sb, hp * dv), lambda b, h, s: (b * nsb + s, h)),
                  cspec, cspec, cspec, cspec,
                  pl.BlockSpec((1, 1, hp * dv), lambda b, h, s: (layer, 0, h))],
        out_specs=pl.BlockSpec((sb, hp * dv), lambda b, h, s: (b * nsb + s, h)),
        out_shape=jax.ShapeDtypeStruct((t, v_width), BF16),
        scratch_shapes=[pltpu.VMEM((hp, dk, dv), F32)],
        compiler_params=_params(("arbitrary", "arbitrary", "arbitrary")),
        name="retention",
    )(qk, qk, v, g, *consts, gn_gain.reshape(gn_gain.shape[0], 1, v_width))


def _router_kernel(u_ref, w_ref, b_ref, cw_ref, ei_ref, rk_ref, cnt_ref, base_ref, *, n_experts):
    @pl.when(pl.program_id(0) == 0)
    def _():
        base_ref[...] = jnp.zeros_like(base_ref)

    logits = jnp.dot(u_ref[...], w_ref[0].astype(BF16), preferred_element_type=F32) + b_ref[0]
    bm = logits.shape[0]
    lane = lax.broadcasted_iota(I32, logits.shape, 1)
    work = jnp.where(lane < n_experts, logits, -jnp.inf)
    vals, hots, idxs = [], [], []
    for _ in range(TOP_K):
        mx = jnp.max(work, axis=-1, keepdims=True)
        idx = jnp.min(jnp.where(work == mx, lane, LANES), axis=-1, keepdims=True)
        hot = lane == idx
        work = jnp.where(hot, -jnp.inf, work)
        vals.append(mx)
        hots.append(hot)
        idxs.append(idx)
    es = [jnp.exp(v - vals[0]) for v in vals]
    inv = 1.0 / functools.reduce(lambda a, b: a + b, es)

    sel = functools.reduce(jnp.logical_or, hots).astype(F32)
    ri = lax.broadcasted_iota(I32, (bm, bm), 0)
    ci = lax.broadcasted_iota(I32, (bm, bm), 1)
    lower = (ci < ri).astype(BF16)
    base = base_ref[0:1, :]
    before = jnp.dot(lower, sel.astype(BF16), preferred_element_type=F32) + base

    cw = jnp.zeros(logits.shape, F32)
    ei = jnp.zeros(logits.shape, I32)
    rk = jnp.zeros(logits.shape, I32)
    for k in range(TOP_K):
        rank = jnp.sum(jnp.where(hots[k], before, 0.0), axis=-1, keepdims=True).astype(I32)
        cw = jnp.where(lane == k, es[k] * inv, cw)
        ei = jnp.where(lane == k, idxs[k], ei)
        rk = jnp.where(lane == k, rank, rk)
    cw_ref[...] = cw
    ei_ref[...] = ei
    rk_ref[...] = rk
    total = base + jnp.sum(sel, axis=0, keepdims=True)
    base_ref[0:1, :] = total
    cnt_ref[...] = jnp.broadcast_to(total, cnt_ref.shape).astype(I32)


def _router(u, rw_pad, rb_pad, layer, n_experts):
    t, d = u.shape
    bm = min(BM, t)
    tok_spec = pl.BlockSpec((bm, LANES), lambda m: (m, 0))
    return pl.pallas_call(
        functools.partial(_router_kernel, n_experts=n_experts),
        grid=(t // bm,),
        in_specs=[pl.BlockSpec((bm, d), lambda m: (m, 0)),
                  pl.BlockSpec((1, d, LANES), lambda m: (layer, 0, 0)),
                  pl.BlockSpec((1, 1, LANES), lambda m: (layer, 0, 0))],
        out_specs=[tok_spec, tok_spec, tok_spec, pl.BlockSpec((8, LANES), lambda m: (0, 0))],
        out_shape=[jax.ShapeDtypeStruct((t, LANES), F32), jax.ShapeDtypeStruct((t, LANES), I32),
                   jax.ShapeDtypeStruct((t, LANES), I32), jax.ShapeDtypeStruct((8, LANES), I32)],
        scratch_shapes=[pltpu.VMEM((8, LANES), F32)],
        compiler_params=_params(("arbitrary",)),
        name="router",
    )(u, rw_pad, rb_pad)


def _invert_kernel(pos_ref, out_ref):
    def clear(i, carry):
        out_ref[i] = 0
        return carry
    lax.fori_loop(0, out_ref.shape[0], clear, 0, unroll=8)

    def place(p, carry):
        out_ref[pos_ref[p]] = p + 1
        return carry
    lax.fori_loop(0, pos_ref.shape[0], place, 0, unroll=8)


def _invert(pos, rows):
    return pl.pallas_call(
        _invert_kernel,
        in_specs=[pl.BlockSpec(memory_space=pltpu.SMEM)],
        out_specs=pl.BlockSpec(memory_space=pltpu.SMEM),
        out_shape=jax.ShapeDtypeStruct((rows,), I32),
        name="invert_rows",
    )(pos)


def _dispatch_plan(ei, rk, cnt, n_experts, t, tm):
    n_tiles = (TOP_K * t) // tm + n_experts
    rows = n_tiles * tm
    counts = cnt[0, :n_experts]
    padded = ((counts + tm - 1) // tm) * tm
    ends = jnp.cumsum(padded)
    starts = ends - padded
    pair_expert = ei[:, :TOP_K].reshape(-1)
    pos = starts[pair_expert] + rk[:, :TOP_K].reshape(-1)
    pair_of_row = _invert(pos, rows)
    valid = pair_of_row > 0
    pair = jnp.maximum(pair_of_row - 1, 0)
    tok, choice = pair // TOP_K, pair % TOP_K
    row = jnp.arange(rows, dtype=I32)
    src = jnp.where(valid, tok, 0)
    dst = jnp.where(valid, choice * t + tok, TOP_K * t + row % tm)
    tile_start = jnp.arange(n_tiles, dtype=I32) * tm
    tile_expert = jnp.minimum(jnp.sum(tile_start[:, None] >= ends[None, :], axis=1), n_experts - 1).astype(I32)
    tile_active = (tile_start < ends[-1]).astype(I32)
    return tile_expert, tile_active, src.reshape(n_tiles, 1, tm), dst.reshape(n_tiles, 1, tm)


def _expert_kernel(te_ref, act_ref, src_ref, srcn_ref, dst_ref, u_hbm, wgu_ref, bgu_ref, wd_ref, bd_ref, ys_hbm,
                   xbuf, obuf, wgu_bf, wd_bf, sem_in, sem_out):
    r = pl.program_id(0)
    last = pl.num_programs(0) - 1
    slot = r % 2
    tm = obuf.shape[0]

    def start_gather(idx_ref, s):
        for i in range(tm):
            pltpu.make_async_copy(u_hbm.at[pl.ds(idx_ref[0, 0, i], 1)], xbuf.at[s, pl.ds(i, 1)], sem_in.at[s]).start()

    def wait_scatter():
        pltpu.make_async_copy(obuf, obuf, sem_out.at[0]).wait()

    active = act_ref[r] != 0
    next_active = (r < last) & (act_ref[jnp.minimum(r + 1, last)] != 0)

    @pl.when(r == 0)
    def _():
        obuf[...] = jnp.zeros_like(obuf)
        spare = pltpu.make_async_copy(obuf, ys_hbm.at[pl.ds(ys_hbm.shape[0] - tm, tm)], sem_out.at[0])
        spare.start()
        spare.wait()

    @pl.when((r == 0) & active)
    def _():
        start_gather(src_ref, 0)

    for s in range(2):
        @pl.when(next_active & (slot == 1 - s))
        def _():
            start_gather(srcn_ref, s)

    @pl.when(active)
    def _():
        @pl.when((r == 0) | (te_ref[r] != te_ref[jnp.maximum(r - 1, 0)]))
        def _():
            _cast_weight(wgu_ref.at[0, 0], wgu_bf)
            _cast_weight(wd_ref.at[0, 0], wd_bf)

        pltpu.make_async_copy(xbuf.at[slot], xbuf.at[slot], sem_in.at[slot]).wait()
        x = jnp.concatenate(_unpack_halves(xbuf[slot]), axis=1).astype(BF16)
        h = jnp.dot(x, wgu_bf[...], preferred_element_type=F32) + bgu_ref[0, 0]
        ff = h.shape[1] // 2
        gate = jnp.minimum(h[:, :ff], SWIGLU_LIMIT)
        up = jnp.clip(h[:, ff:], -SWIGLU_LIMIT, SWIGLU_LIMIT)
        act = (up + 1.0) * gate * jax.nn.sigmoid(SWIGLU_ALPHA * gate)
        y = jnp.dot(act.astype(BF16), wd_bf[...], preferred_element_type=F32) + bd_ref[0, 0]

        @pl.when(r > 0)
        def _():
            wait_scatter()

        obuf[...] = _pack_halves(y)

        for i in range(tm):
            pltpu.make_async_copy(obuf.at[pl.ds(i, 1)], ys_hbm.at[pl.ds(dst_ref[0, 0, i], 1)], sem_out.at[0]).start()

        @pl.when(jnp.logical_not(next_active))
        def _():
            wait_scatter()


def _experts(u32, w_gu, b_gu, w_down, b_down, layer, plan, tm):
    tile_expert, tile_active, src, dst = plan
    t = u32.shape[0]
    d = w_gu.shape[2]
    n_tiles = src.shape[0]
    n_exp, ff2 = w_gu.shape[1], w_gu.shape[3]
    ff = ff2 // 2
    smem_spec = lambda f: pl.BlockSpec((1, 1, tm), f, memory_space=pltpu.SMEM)
    grid_spec = pltpu.PrefetchScalarGridSpec(
        num_scalar_prefetch=2,
        grid=(n_tiles,),
        in_specs=[smem_spec(lambda r, te, ac: (r, 0, 0)),
                  smem_spec(lambda r, te, ac: (jnp.minimum(r + 1, n_tiles - 1), 0, 0)),
                  smem_spec(lambda r, te, ac: (r, 0, 0)),
                  pl.BlockSpec(memory_space=pl.ANY),
                  pl.BlockSpec((1, 1, d, ff2), lambda r, te, ac: (layer, te[r], 0, 0)),
                  pl.BlockSpec((1, 1, 1, ff2), lambda r, te, ac: (layer, te[r], 0, 0)),
                  pl.BlockSpec((1, 1, ff, d), lambda r, te, ac: (layer, te[r], 0, 0)),
                  pl.BlockSpec((1, 1, 1, d), lambda r, te, ac: (layer, te[r], 0, 0))],
        out_specs=pl.BlockSpec(memory_space=pl.ANY),
        scratch_shapes=[pltpu.VMEM((2, tm, d // 2), U32), pltpu.VMEM((tm, d // 2), U32),
                        pltpu.VMEM((d, ff2), BF16), pltpu.VMEM((ff, d), BF16),
                        pltpu.SemaphoreType.DMA((2,)), pltpu.SemaphoreType.DMA((1,))])
    return pl.pallas_call(
        _expert_kernel,
        grid_spec=grid_spec,
        out_shape=jax.ShapeDtypeStruct((TOP_K * t + tm, d // 2), U32),
        compiler_params=_params(("arbitrary",)),
        name="moe_experts",
    )(tile_expert, tile_active, src, src, dst, u32, w_gu, b_gu.reshape(b_gu.shape[0], n_exp, 1, ff2), w_down,
      b_down.reshape(b_down.shape[0], n_exp, 1, d))


def _attn_rope_tables(pos):
    half = ROPE_DIM // 2
    inv = ROPE_THETA ** (-jnp.arange(0, ROPE_DIM, 2, dtype=F32) / ROPE_DIM)
    ang = pos[:, None] * inv
    cos, sin = jnp.cos(ang), jnp.sin(ang)
    t = pos.shape[0]
    ones = jnp.ones((t, HEAD_DIM - ROPE_DIM), F32)
    zeros_h = jnp.zeros((t, half), F32)
    zeros_r = jnp.zeros((t, HEAD_DIM - ROPE_DIM), F32)
    c = jnp.concatenate([cos, cos, ones], axis=1)
    s_hi = jnp.concatenate([-sin, zeros_h, zeros_r], axis=1)
    s_lo = jnp.concatenate([zeros_h, sin, zeros_r], axis=1)
    rep = LANES // HEAD_DIM
    return tuple(jnp.tile(a, (1, rep)) for a in (c, s_hi, s_lo))


def _ret_rot_tables(pos, key_dim):
    inv = RET_ROT_BASE ** (-jnp.linspace(0.0, 1.0, key_dim // 2, dtype=F32))
    ang = pos[:, None] * inv
    cos, sin = jnp.cos(ang), jnp.sin(ang)
    c = jnp.repeat(cos, 2, axis=1)
    s = jnp.stack([-sin, sin], axis=-1).reshape(pos.shape[0], key_dim)
    return c, s


def _ret_decay_tables(heads):
    c = RET_CHUNK
    log_decay = jnp.log1p(-jnp.exp2(-5.0 - jnp.arange(heads, dtype=F32)))
    idx = jnp.arange(c, dtype=F32)
    rel = idx[:, None] - idx[None, :]
    decay_intra = jnp.where(rel >= 0, jnp.exp(jnp.maximum(rel, 0.0)[None] * log_decay[:, None, None]), 0.0)
    xi = jnp.exp((idx + 1.0)[None, :] * log_decay[:, None])
    zeta = jnp.exp((c - 1.0 - idx)[None, :] * log_decay[:, None])
    decay_chunk = jnp.exp(c * log_decay)
    lanes = lambda a: jnp.broadcast_to(a[:, :, None], (heads, c, LANES))
    return (decay_intra, lanes(xi), lanes(zeta), jnp.broadcast_to(decay_chunk[:, None, None], (heads, c, LANES)))


def kernel(x, c, positions, mod_w, mod_b, mod_layer, ln_gain, ln_bias, attn_w_qkv, attn_b_qkv, attn_sinks, attn_w_o,
           attn_b_o, ret_w_qkvg, ret_gn_gain, ret_w_o, router_w, router_b, expert_w_gu, expert_b_gu, expert_w_down,
           expert_b_down):
    bsz, seq, d = x.shape
    t = bsz * seq
    depth = mod_layer.shape[0]
    alpha = (2 * depth) ** 0.25
    n_exp = router_w.shape[2]
    q_dim = attn_w_o.shape[1]
    kv_dim = (attn_w_qkv.shape[2] - q_dim) // 2
    v_width = ret_w_o.shape[1]
    key_dim = d // RET_HEADS
    moe_tm = min(MOE_TM, seq)
    assert n_exp <= LANES and seq % WINDOW == 0 and seq % RET_CHUNK == 0 and (TOP_K * t) % moe_tm == 0

    mod = _mod_proj(c, mod_w, mod_b).reshape(bsz, N_MOD, d)
    m_all = mod[None] + mod_layer[:, None]
    mrow = lambda i, j: m_all[i, :, j][:, None, :]

    pos = positions.astype(F32).reshape(t)
    attn_tabs = _attn_rope_tables(pos)
    ret_tabs = _ret_rot_tables(pos, key_dim)
    ret_consts = _ret_decay_tables(RET_HEADS)

    rw_pad = jnp.zeros((depth, d, LANES), F32).at[:, :, :n_exp].set(router_w)
    rb_pad = jnp.zeros((depth, 1, LANES), F32).at[:, 0, :n_exp].set(router_b)

    xcur = x.reshape(t, d)
    (u,) = _ln_mod(xcur, seq, mod=(mrow(0, 1), mrow(0, 0)))
    for i in range(depth):
        j = i // 2
        if i % 2 == 0:
            qk = _proj(u, attn_w_qkv, j, 0, q_dim + kv_dim, "rope", bias=attn_b_qkv, tabs=attn_tabs)
            v = _proj(u, attn_w_qkv, j, q_dim + kv_dim, kv_dim, "plain", bias=attn_b_qkv)
            o = _attention(qk, v, attn_sinks[j], seq, q_dim)
            z = _residual_mm(o, attn_w_o, j, xcur, mrow(i, 2), seq, alpha, bias=attn_b_o)
        else:
            qk = _proj(u, ret_w_qkvg, j, 0, 2 * d, "rot", tabs=ret_tabs, k_from_col=d, k_scale=key_dim ** -0.5)
            v = _proj(u, ret_w_qkvg, j, 2 * d, v_width, "plain")
            g = _proj(u, ret_w_qkvg, j, 2 * d + v_width, v_width, "silu")
            o = _retention(qk, v, g, ret_gn_gain, j, ret_consts, seq, d)
            z = _residual_mm(o, ret_w_o, j, xcur, mrow(i, 2), seq, alpha)
        xcur, u, u32 = _ln_mod(z, seq, ln=(ln_gain, ln_bias, i, 0), mod=(mrow(i, 4), mrow(i, 3)), emit_u32=True)

        cw, ei, rk, cnt = _router(u, rw_pad, rb_pad, i, n_exp)
        plan = _dispatch_plan(ei, rk, cnt, n_exp, t, moe_tm)
        ys = _experts(u32, expert_w_gu, expert_b_gu, expert_w_down, expert_b_down, i, plan, moe_tm)
        combine = (ys, cw, xcur, mrow(i, 5), alpha)
        if i + 1 < depth:
            xcur, u = _ln_mod(None, seq, ln=(ln_gain, ln_bias, i, 1), mod=(mrow(i + 1, 1), mrow(i + 1, 0)),
                              combine=combine)
        else:
            (xcur,) = _ln_mod(None, seq, ln=(ln_gain, ln_bias, i, 1), combine=combine)
    return xcur.reshape(bsz, seq, d)
```
